```python
import jax, jax.numpy as jnp
from jax import lax
import numpy as np

D_MODEL = 1024
BATCH = 1
SEQ = 16384
DEPTH = 2

HEAD_DIM = 64
CONV_HEADS = 6
CONV_DIM = CONV_HEADS * HEAD_DIM
FOURIER_HEADS = 4
FOURIER_DIM = FOURIER_HEADS * HEAD_DIM
POOL_WINDOWS = (2, 4, 8, 16)
POOL_GROUPS = len(POOL_WINDOWS)
POOL_GROUP_DIM = 96
POOL_DIM = POOL_GROUPS * POOL_GROUP_DIM
MIX_DIM = CONV_DIM + FOURIER_DIM + POOL_DIM
IN_PROJ_DIM = 3 * CONV_DIM + FOURIER_DIM + POOL_DIM
CONV_WIDTH = 3
D_FF = ((8 * D_MODEL // 3 + 255) // 256) * 256
EPS = 1e-6

kernel_name = "hybrid_conv_fourier_pool_encoder"


def rmsnorm(x, g):
    xf = x.astype(jnp.float32)
    y = xf * lax.rsqrt(jnp.mean(xf * xf, axis=-1, keepdims=True) + EPS)
    return (y * g.astype(jnp.float32)).astype(x.dtype)


def short_gated_conv(b_gate, c_gate, h, w_conv):
    u = c_gate * h
    up = jnp.pad(u, ((0, 0), (1, 1), (0, 0)))
    s = u.shape[1]
    conv = (w_conv[0] * up[:, 0:s] + w_conv[1] * up[:, 1:s + 1] + w_conv[2] * up[:, 2:s + 2])
    return b_gate * conv


def fourier_mix(u, w_f):
    b, s, _ = u.shape
    uh = u.astype(jnp.float32).reshape(b, s, FOURIER_HEADS, HEAD_DIM)
    f = jnp.real(jnp.fft.fft2(uh, axes=(1, 3), norm="ortho")).astype(u.dtype)
    y = jnp.einsum("bshd,hde->bshe", f, w_f)
    return y.reshape(b, s, FOURIER_DIM)


def multiscale_pool(u, w_p, scale):
    b, s, _ = u.shape
    ug = u.reshape(b, s, POOL_GROUPS, POOL_GROUP_DIM)
    uf = ug.astype(jnp.float32)
    cs = jnp.concatenate([jnp.zeros((b, 1, POOL_GROUPS, POOL_GROUP_DIM), jnp.float32),
                          jnp.cumsum(uf, axis=1)], axis=1)
    t = jnp.arange(s)
    outs = []
    for g, w in enumerate(POOL_WINDOWS):
        lo = jnp.clip(t - w // 2, 0, s)
        hi = jnp.clip(t + w // 2, 0, s)
        cnt = (hi - lo).astype(jnp.float32)[None, :, None]
        mean = (cs[:, hi, g] - cs[:, lo, g]) / cnt
        outs.append(mean - uf[:, :, g])
    pooled = jnp.stack(outs, axis=2).astype(u.dtype)
    y = jnp.einsum("bsgd,gde->bsge", pooled, w_p).reshape(b, s, POOL_DIM)
    return y * scale


def swiglu(x, w1, w3, w2):
    return (jax.nn.silu(x @ w1) * (x @ w3)) @ w2


def setup_inputs(seed: int = 0) -> dict:
    key = jax.random.key(seed)
    ks = jax.random.split(key, 16)
    f32 = jnp.float32
    nrm = lambda k, shape, fan: jax.random.normal(k, shape, f32) * (fan ** -0.5)
    return {
        "x": jax.random.normal(ks[0], (BATCH, SEQ, D_MODEL), f32),
        "g_mix": 1.0 + 0.02 * jax.random.normal(ks[1], (DEPTH, D_MODEL), f32),
        "w_in": nrm(ks[2], (DEPTH, D_MODEL, IN_PROJ_DIM), D_MODEL),
        "w_conv": nrm(ks[3], (DEPTH, CONV_WIDTH, CONV_DIM), CONV_WIDTH),
        "w_fourier": nrm(ks[4], (DEPTH, FOURIER_HEADS, HEAD_DIM, HEAD_DIM), HEAD_DIM),
        "w_pool": nrm(ks[5], (DEPTH, POOL_GROUPS, POOL_GROUP_DIM, POOL_GROUP_DIM), POOL_GROUP_DIM),
        "pool_scale": 1.0 + 0.02 * jax.random.normal(ks[6], (DEPTH, POOL_DIM), f32),
        "w_out": nrm(ks[7], (DEPTH, MIX_DIM, D_MODEL), MIX_DIM),
        "g_ffn": 1.0 + 0.02 * jax.random.normal(ks[8], (DEPTH, D_MODEL), f32),
        "w1": nrm(ks[9], (DEPTH, D_MODEL, D_FF), D_MODEL),
        "w3": nrm(ks[10], (DEPTH, D_MODEL, D_FF), D_MODEL),
        "w2": nrm(ks[11], (DEPTH, D_FF, D_MODEL), D_FF),
        "g_final": 1.0 + 0.02 * jax.random.normal(ks[12], (D_MODEL,), f32),
    }


def reference(x, g_mix, w_in, w_conv, w_fourier, w_pool, pool_scale, w_out,
              g_ffn, w1, w3, w2, g_final):
    c0 = CONV_DIM
    for l in range(DEPTH):
        xn = rmsnorm(x, g_mix[l])
        z = xn @ w_in[l]
        b_gate = z[..., 0:c0]
        c_gate = z[..., c0:2 * c0]
        h = z[..., 2 * c0:3 * c0]
        u_f = z[..., 3 * c0:3 * c0 + FOURIER_DIM]
        u_p = z[..., 3 * c0 + FOURIER_DIM:]
        y_a = short_gated_conv(b_gate, c_gate, h, w_conv[l])
        y_b = fourier_mix(u_f, w_fourier[l])
        y_c = multiscale_pool(u_p, w_pool[l], pool_scale[l])
        y = jnp.concatenate([y_a, y_b, y_c], axis=-1)
        x = x + y @ w_out[l]
        x = x + swiglu(rmsnorm(x, g_ffn[l]), w1[l], w3[l], w2[l])
    return rmsnorm(x, g_final)
```

```python
import functools

import numpy as np
import jax
import jax.numpy as jnp
from jax import lax
from jax.experimental import pallas as pl
from jax.experimental.pallas import tpu as pltpu

D_MODEL = 1024
SEQ = 16384
DEPTH = 2
HEAD_DIM = 64
CONV_DIM = 384
FOURIER_HEADS = 4
FOURIER_DIM = 256
POOL_WINDOWS = (2, 4, 8, 16)
POOL_GROUP_DIM = 96
POOL_DIM = 384
IN_PROJ_DIM = 3 * CONV_DIM + FOURIER_DIM + POOL_DIM
D_FF = 2816
EPS = 1e-6

RADIX = 128
YAC_DIM = CONV_DIM + POOL_DIM
Z_DIM = 2 * FOURIER_DIM

TM_IN = 512
HALO = 16
DFT_COLS = 8
OUT_SLABS = 4
FF_CHUNK = 1408
VMEM_LIMIT = 56 * 1024 * 1024

F32 = jnp.float32
BF16 = jnp.bfloat16


def _dot(a, b):
    return jnp.dot(a, b, preferred_element_type=F32)


def _rms(v, g):
    ms = jnp.mean(v * v, axis=-1, keepdims=True)
    return v * lax.rsqrt(ms + EPS) * g


def _tables():
    r = np.arange(RADIX, dtype=np.float64)
    ang128 = 2.0 * np.pi * np.outer(r, r) / RADIX
    angs = 2.0 * np.pi * np.outer(r, r) / SEQ
    f_stage1 = np.concatenate([np.cos(ang128), -np.sin(ang128)], axis=0)
    e = np.arange(HEAD_DIM, dtype=np.float64)
    angc = 2.0 * np.pi * np.outer(e, e) / HEAD_DIM
    norm = 1.0 / np.sqrt(float(SEQ) * HEAD_DIM)
    eye = np.eye(FOURIER_HEADS)
    cbd = np.kron(eye, np.cos(angc)) * norm
    sbd = np.kron(eye, -np.sin(angc)) * norm
    return dict(
        f_stage1=f_stage1.astype(np.float32),
        cphi=np.cos(ang128).astype(np.float32), sphi=np.sin(ang128).astype(np.float32),
        cth=np.cos(angs).astype(np.float32), sth=np.sin(angs).astype(np.float32),
        chan=np.stack([cbd, sbd]).astype(np.float32),
    )


def _fold_kernel(chan_ref, wf_ref, ab_ref):
    wf = wf_ref[0]
    hp = lax.Precision.HIGHEST
    ab_ref[0, :, 0:FOURIER_DIM] = jnp.dot(chan_ref[0], wf, precision=hp, preferred_element_type=F32).astype(BF16)
    ab_ref[0, :, FOURIER_DIM:Z_DIM] = jnp.dot(chan_ref[1], wf, precision=hp, preferred_element_type=F32).astype(BF16)


def _fold_fourier(chan, wf_bd):
    return pl.pallas_call(
        _fold_kernel,
        grid=(DEPTH,),
        in_specs=[pl.BlockSpec((2, FOURIER_DIM, FOURIER_DIM), lambda l: (0, 0, 0)),
                  pl.BlockSpec((1, FOURIER_DIM, FOURIER_DIM), lambda l: (l, 0, 0))],
        out_specs=pl.BlockSpec((1, FOURIER_DIM, Z_DIM), lambda l: (l, 0, 0)),
        out_shape=jax.ShapeDtypeStruct((DEPTH, FOURIER_DIM, Z_DIM), BF16),
        name="fold_fourier",
    )(chan, wf_bd)


def _mix_in_kernel(x_ref, xprev_ref, xnext_ref, g_ref, win_ref, wconv_ref, ab_ref, wp_ref, ps_ref,
                   yac_ref, z_ref, xe_ref):
    i = pl.program_id(0)
    n = pl.num_programs(0)
    rows = TM_IN + 2 * HALO
    g = g_ref[...]
    prev = jnp.where(i > 0, _rms(xprev_ref[...], g), 0.0)
    nxt = jnp.where(i < n - 1, _rms(xnext_ref[...], g), 0.0)
    xe_ref[0:HALO, :] = prev.astype(BF16)
    xe_ref[HALO:HALO + TM_IN, :] = _rms(x_ref[...], g).astype(BF16)
    xe_ref[HALO + TM_IN:rows, :] = nxt.astype(BF16)

    xe = xe_ref[...]
    xm = xe_ref[HALO:HALO + TM_IN, :]
    c0 = CONV_DIM
    zb = _dot(xm, win_ref[:, 0:c0])
    zch = _dot(xe, win_ref[:, c0:3 * c0])
    zf = _dot(xm, win_ref[:, 3 * c0:3 * c0 + FOURIER_DIM])
    zp = _dot(xe, win_ref[:, 3 * c0 + FOURIER_DIM:IN_PROJ_DIM])

    u = zch[:, 0:c0] * zch[:, c0:2 * c0]
    conv = (wconv_ref[0:1, :] * pltpu.roll(u, 1, 0) + wconv_ref[1:2, :] * u
            + wconv_ref[2:3, :] * pltpu.roll(u, rows - 1, 0))
    ya = zb * conv[HALO:HALO + TM_IN, :]
    yac_ref[:, 0:c0] = ya.astype(BF16)

    s1 = zp + pltpu.roll(zp, 1, 0)
    s2 = pltpu.roll(s1, 1, 0) + pltpu.roll(s1, rows - 1, 0)
    s4 = pltpu.roll(s2, 2, 0) + pltpu.roll(s2, rows - 2, 0)
    s8 = pltpu.roll(s4, 4, 0) + pltpu.roll(s4, rows - 4, 0)
    mid = slice(HALO, HALO + TM_IN)
    lane = lax.broadcasted_iota(jnp.int32, (TM_IN, POOL_DIM), 1)
    g0 = lane < POOL_GROUP_DIM
    g1 = lane < 2 * POOL_GROUP_DIM
    g2 = lane < 3 * POOL_GROUP_DIM
    half = jnp.where(g0, POOL_WINDOWS[0] // 2,
                     jnp.where(g1, POOL_WINDOWS[1] // 2,
                               jnp.where(g2, POOL_WINDOWS[2] // 2, POOL_WINDOWS[3] // 2)))
    wsum = jnp.where(g0, s1[mid], jnp.where(g1, s2[mid], jnp.where(g2, s4[mid], s8[mid])))
    t = i * TM_IN + lax.broadcasted_iota(jnp.int32, (TM_IN, POOL_DIM), 0)
    cnt = (jnp.minimum(t + half, SEQ) - jnp.maximum(t - half, 0)).astype(F32)
    pooled = wsum / cnt - zp[mid]
    yc = _dot(pooled.astype(BF16), wp_ref[...]) * ps_ref[...]
    yac_ref[:, c0:YAC_DIM] = yc.astype(BF16)

    z_ref[...] = _dot(zf.astype(BF16), ab_ref[...]).astype(BF16)


def _mix_in(x, g, win, wconv, ab, wp, ps):
    n = SEQ // TM_IN
    hb = TM_IN // HALO
    const = lambda i: (0, 0)
    return pl.pallas_call(
        _mix_in_kernel,
        grid=(n,),
        in_specs=[
            pl.BlockSpec((TM_IN, D_MODEL), lambda i: (i, 0)),
            pl.BlockSpec((HALO, D_MODEL), lambda i: (jnp.maximum(i * hb - 1, 0), 0)),
            pl.BlockSpec((HALO, D_MODEL), lambda i: (jnp.minimum((i + 1) * hb, SEQ // HALO - 1), 0)),
            pl.BlockSpec((1, D_MODEL), const),
            pl.BlockSpec((D_MODEL, IN_PROJ_DIM), const, pipeline_mode=pl.Buffered(1)),
            pl.BlockSpec((3, CONV_DIM), const),
            pl.BlockSpec((FOURIER_DIM, Z_DIM), const),
            pl.BlockSpec((POOL_DIM, POOL_DIM), const),
            pl.BlockSpec((1, POOL_DIM), const),
        ],
        out_specs=[pl.BlockSpec((TM_IN, YAC_DIM), lambda i: (i, 0)),
                   pl.BlockSpec((TM_IN, Z_DIM), lambda i: (i, 0))],
        out_shape=[jax.ShapeDtypeStruct((SEQ, YAC_DIM), BF16),
                   jax.ShapeDtypeStruct((SEQ, Z_DIM), BF16)],
        scratch_shapes=[pltpu.VMEM((TM_IN + 2 * HALO, D_MODEL), BF16)],
        compiler_params=pltpu.CompilerParams(dimension_semantics=("arbitrary",),
                                             vmem_limit_bytes=VMEM_LIMIT),
        name="mix_in",
    )(x, x, x, g, win, wconv, ab, wp, ps)


def _dft_a_kernel(f_ref, z_ref, t_ref):
    res = _dot(f_ref[...], z_ref[...])
    h = FOURIER_DIM
    for j in range(DFT_COLS):
        re = slice(j * Z_DIM, j * Z_DIM + h)
        im = slice(j * Z_DIM + h, (j + 1) * Z_DIM)
        rows = slice(j * RADIX, (j + 1) * RADIX)
        t_ref[rows, 0:h] = (res[0:RADIX, re] - res[RADIX:2 * RADIX, im]).astype(BF16)
        t_ref[rows, h:Z_DIM] = (res[0:RADIX, im] + res[RADIX:2 * RADIX, re]).astype(BF16)


def _dft_a(f_stage1, z):
    z2 = z.reshape(RADIX, RADIX * Z_DIM)
    return pl.pallas_call(
        _dft_a_kernel,
        grid=(RADIX // DFT_COLS,),
        in_specs=[pl.BlockSpec((2 * RADIX, RADIX), lambda t: (0, 0)),
                  pl.BlockSpec((RADIX, DFT_COLS * Z_DIM), lambda t: (0, t))],
        out_specs=pl.BlockSpec((DFT_COLS * RADIX, Z_DIM), lambda t: (t, 0)),
        out_shape=jax.ShapeDtypeStruct((SEQ, Z_DIM), BF16),
        compiler_params=pltpu.CompilerParams(dimension_semantics=("arbitrary",)),
        name="dft_a",
    )(f_stage1, z2)


def _mix_out_kernel(x_ref, yac_ref, t_ref, cphi_ref, sphi_ref, cth_ref, sth_ref, wout_ref, gffn_ref,
                    w1_ref, w3_ref, w2_ref, gfin_ref, out_ref, xbuf, ybuf, *, final_norm):
    step = pl.program_id(0)
    c0 = CONV_DIM
    h = FOURIER_DIM
    cphi = cphi_ref[...]
    sphi = sphi_ref[...]
    for j in range(OUT_SLABS):
        rows = slice(j * RADIX, (j + 1) * RADIX)
        c = step * OUT_SLABS + j
        cth = cth_ref[pl.ds(c, 1), :]
        sth = sth_ref[pl.ds(c, 1), :]
        mcos = (cphi * cth - sphi * sth).astype(BF16)
        msin = (sphi * cth + cphi * sth).astype(BF16)
        yb = (_dot(mcos, t_ref[:, j * Z_DIM:j * Z_DIM + h])
              + _dot(msin, t_ref[:, j * Z_DIM + h:(j + 1) * Z_DIM]))
        ybuf[rows, 0:c0] = yac_ref[:, j * YAC_DIM:j * YAC_DIM + c0]
        ybuf[rows, c0:c0 + h] = yb.astype(BF16)
        ybuf[rows, c0 + h:D_MODEL] = yac_ref[:, j * YAC_DIM + c0:(j + 1) * YAC_DIM]
        xbuf[rows, :] = x_ref[:, j * D_MODEL:(j + 1) * D_MODEL]

    x1 = xbuf[...] + _dot(ybuf[...], wout_ref[...])
    xn = _rms(x1, gffn_ref[...]).astype(BF16)
    acc = None
    for k in range(D_FF // FF_CHUNK):
        cols = slice(k * FF_CHUNK, (k + 1) * FF_CHUNK)
        gate = _dot(xn, w1_ref[:, cols])
        up = _dot(xn, w3_ref[:, cols])
        hid = (gate * (1.0 / (1.0 + jnp.exp(-gate))) * up).astype(BF16)
        part = _dot(hid, w2_ref[cols, :])
        acc = part if acc is None else acc + part
    x2 = x1 + acc
    if final_norm:
        x2 = _rms(x2, gfin_ref[...])
    for j in range(OUT_SLABS):
        out_ref[:, j * D_MODEL:(j + 1) * D_MODEL] = x2[j * RADIX:(j + 1) * RADIX, :]


def _mix_out(x, yac, t, tabs, wout, gffn, w1, w3, w2, gfin, final_norm):
    x2 = x.reshape(RADIX, RADIX * D_MODEL)
    yac2 = yac.reshape(RADIX, RADIX * YAC_DIM)
    t2 = t.reshape(RADIX, RADIX * Z_DIM)
    const = lambda s: (0, 0)
    once = dict(pipeline_mode=pl.Buffered(1))
    tm = OUT_SLABS * RADIX
    out = pl.pallas_call(
        functools.partial(_mix_out_kernel, final_norm=final_norm),
        grid=(RADIX // OUT_SLABS,),
        in_specs=[
            pl.BlockSpec((RADIX, OUT_SLABS * D_MODEL), lambda s: (0, s)),
            pl.BlockSpec((RADIX, OUT_SLABS * YAC_DIM), lambda s: (0, s)),
            pl.BlockSpec((RADIX, OUT_SLABS * Z_DIM), lambda s: (0, s)),
            pl.BlockSpec((RADIX, RADIX), const),
            pl.BlockSpec((RADIX, RADIX), const),
            pl.BlockSpec((RADIX, RADIX), const),
            pl.BlockSpec((RADIX, RADIX), const),
            pl.BlockSpec((D_MODEL, D_MODEL), const, **once),
            pl.BlockSpec((1, D_MODEL), const),
            pl.BlockSpec((D_MODEL, D_FF), const, **once),
            pl.BlockSpec((D_MODEL, D_FF), const, **once),
            pl.BlockSpec((D_FF, D_MODEL), const, **once),
            pl.BlockSpec((1, D_MODEL), const),
        ],
        out_specs=pl.BlockSpec((RADIX, OUT_SLABS * D_MODEL), lambda s: (0, s)),
        out_shape=jax.ShapeDtypeStruct((RADIX, RADIX * D_MODEL), F32),
        scratch_shapes=[pltpu.VMEM((tm, D_MODEL), F32), pltpu.VMEM((tm, D_MODEL), BF16)],
        compiler_params=pltpu.CompilerParams(dimension_semantics=("arbitrary",),
                                             vmem_limit_bytes=VMEM_LIMIT),
        name="mix_out",
    )(x2, yac2, t2, tabs["cphi"], tabs["sphi"], tabs["cth"], tabs["sth"], wout, gffn, w1, w3, w2, gfin)
    return out.reshape(SEQ, D_MODEL)


def kernel(x, g_mix, w_in, w_conv, w_fourier, w_pool, pool_scale, w_out, g_ffn, w1, w3, w2, g_final):
    tabs = {k: jnp.asarray(v) for k, v in _tables().items()}
    block_diag = jax.scipy.linalg.block_diag
    wf_bd = jnp.stack([block_diag(*[w_fourier[l, h] for h in range(FOURIER_HEADS)]) for l in range(DEPTH)])
    wp_bd = jnp.stack([block_diag(*[w_pool[l, g] for g in range(len(POOL_WINDOWS))]) for l in range(DEPTH)])
    ab = _fold_fourier(tabs["chan"], wf_bd)
    f_stage1 = tabs["f_stage1"].astype(BF16)

    xs = x.reshape(SEQ, D_MODEL)
    for l in range(DEPTH):
        yac, z = _mix_in(xs, g_mix[l][None, :], w_in[l].astype(BF16), w_conv[l], ab[l],
                         wp_bd[l].astype(BF16), pool_scale[l][None, :])
        t = _dft_a(f_stage1, z)
        xs = _mix_out(xs, yac, t, tabs, w_out[l].astype(BF16), g_ffn[l][None, :],
                      w1[l].astype(BF16), w3[l].astype(BF16), w2[l].astype(BF16),
                      g_final[None, :], final_norm=(l == DEPTH - 1))
    return xs.reshape(1, SEQ, D_MODEL)
```

```python
import functools

import numpy as np
import jax
import jax.numpy as jnp
from jax import lax
from jax.experimental import pallas as pl
from jax.experimental.pallas import tpu as pltpu

D_MODEL = 1024
SEQ = 16384
DEPTH = 2
HEAD_DIM = 64
CONV_DIM = 384
FOURIER_HEADS = 4
FOURIER_DIM = 256
POOL_WINDOWS = (2, 4, 8, 16)
POOL_GROUP_DIM = 96
POOL_DIM = 384
IN_PROJ_DIM = 3 * CONV_DIM + FOURIER_DIM + POOL_DIM
D_FF = 2816
EPS = 1e-6

RADIX = 128
YAC_DIM = CONV_DIM + POOL_DIM
Z_DIM = 2 * FOURIER_DIM
SUBLANES = 8

TM_IN = 512
HALO = 16
D_SPLIT = 2
D_ROWS = RADIX // D_SPLIT
TM_OUT = D_ROWS * SUBLANES
FF_CHUNK = 1408
VMEM_LIMIT = 56 * 1024 * 1024

F32 = jnp.float32
BF16 = jnp.bfloat16


def _dot(a, b):
    return jnp.dot(a, b, preferred_element_type=F32)


def _rms(v, g):
    ms = jnp.mean(v * v, axis=-1, keepdims=True)
    return v * lax.rsqrt(ms + EPS) * g


def _tables():
    r = np.arange(RADIX, dtype=np.float64)
    ang128 = 2.0 * np.pi * np.outer(r, r) / RADIX
    angs = 2.0 * np.pi * np.outer(r, r) / SEQ
    f_stage1 = np.concatenate([np.cos(ang128), -np.sin(ang128)], axis=0)
    e = np.arange(HEAD_DIM, dtype=np.float64)
    angc = 2.0 * np.pi * np.outer(e, e) / HEAD_DIM
    norm = 1.0 / np.sqrt(float(SEQ) * HEAD_DIM)
    eye = np.eye(FOURIER_HEADS)
    cbd = np.kron(eye, np.cos(angc)) * norm
    sbd = np.kron(eye, -np.sin(angc)) * norm
    return dict(
        f_stage1=f_stage1.astype(np.float32),
        cphi=np.cos(ang128).astype(np.float32), sphi=np.sin(ang128).astype(np.float32),
        cth=np.cos(angs).astype(np.float32), sth=np.sin(angs).astype(np.float32),
        chan=np.stack([cbd, sbd]).astype(np.float32),
    )


def _fold_kernel(chan_ref, wf_ref, ab_ref):
    wf = wf_ref[0]
    hp = lax.Precision.HIGHEST
    ab_ref[0, :, 0:FOURIER_DIM] = jnp.dot(chan_ref[0], wf, precision=hp, preferred_element_type=F32).astype(BF16)
    ab_ref[0, :, FOURIER_DIM:Z_DIM] = jnp.dot(chan_ref[1], wf, precision=hp, preferred_element_type=F32).astype(BF16)


def _fold_fourier(chan, wf_bd):
    return pl.pallas_call(
        _fold_kernel,
        grid=(DEPTH,),
        in_specs=[pl.BlockSpec((2, FOURIER_DIM, FOURIER_DIM), lambda l: (0, 0, 0)),
                  pl.BlockSpec((1, FOURIER_DIM, FOURIER_DIM), lambda l: (l, 0, 0))],
        out_specs=pl.BlockSpec((1, FOURIER_DIM, Z_DIM), lambda l: (l, 0, 0)),
        out_shape=jax.ShapeDtypeStruct((DEPTH, FOURIER_DIM, Z_DIM), BF16),
        name="fold_fourier",
    )(chan, wf_bd)


def _mix_in_kernel(x_ref, xprev_ref, xnext_ref, g_ref, win_ref, wconv_ref, ab_ref, wp_ref, ps_ref,
                   yac_ref, z_ref, xe_ref):
    i = pl.program_id(0)
    n = pl.num_programs(0)
    rows = TM_IN + 2 * HALO
    g = g_ref[...]
    prev = jnp.where(i > 0, _rms(xprev_ref[...], g), 0.0)
    nxt = jnp.where(i < n - 1, _rms(xnext_ref[...], g), 0.0)
    xe_ref[0:HALO, :] = prev.astype(BF16)
    xe_ref[HALO:HALO + TM_IN, :] = _rms(x_ref[...], g).astype(BF16)
    xe_ref[HALO + TM_IN:rows, :] = nxt.astype(BF16)

    xe = xe_ref[...]
    xm = xe_ref[HALO:HALO + TM_IN, :]
    c0 = CONV_DIM
    zb = _dot(xm, win_ref[:, 0:c0])
    zch = _dot(xe, win_ref[:, c0:3 * c0])
    zf = _dot(xm, win_ref[:, 3 * c0:3 * c0 + FOURIER_DIM])
    zp = _dot(xe, win_ref[:, 3 * c0 + FOURIER_DIM:IN_PROJ_DIM])

    u = zch[:, 0:c0] * zch[:, c0:2 * c0]
    conv = (wconv_ref[0:1, :] * pltpu.roll(u, 1, 0) + wconv_ref[1:2, :] * u
            + wconv_ref[2:3, :] * pltpu.roll(u, rows - 1, 0))
    yac_ref[:, 0:c0] = zb * conv[HALO:HALO + TM_IN, :]

    s1 = zp + pltpu.roll(zp, 1, 0)
    s2 = pltpu.roll(s1, 1, 0) + pltpu.roll(s1, rows - 1, 0)
    s4 = pltpu.roll(s2, 2, 0) + pltpu.roll(s2, rows - 2, 0)
    s8 = pltpu.roll(s4, 4, 0) + pltpu.roll(s4, rows - 4, 0)
    mid = slice(HALO, HALO + TM_IN)
    lane = lax.broadcasted_iota(jnp.int32, (TM_IN, POOL_DIM), 1)
    g0 = lane < POOL_GROUP_DIM
    g1 = lane < 2 * POOL_GROUP_DIM
    g2 = lane < 3 * POOL_GROUP_DIM
    half = jnp.where(g0, POOL_WINDOWS[0] // 2,
                     jnp.where(g1, POOL_WINDOWS[1] // 2,
                               jnp.where(g2, POOL_WINDOWS[2] // 2, POOL_WINDOWS[3] // 2)))
    wsum = jnp.where(g0, s1[mid], jnp.where(g1, s2[mid], jnp.where(g2, s4[mid], s8[mid])))
    t = i * TM_IN + lax.broadcasted_iota(jnp.int32, (TM_IN, POOL_DIM), 0)
    cnt = (jnp.minimum(t + half, SEQ) - jnp.maximum(t - half, 0)).astype(F32)
    pooled = wsum / cnt - zp[mid]
    yac_ref[:, c0:YAC_DIM] = _dot(pooled.astype(BF16), wp_ref[...]) * ps_ref[...]

    z_ref[...] = _dot(zf.astype(BF16), ab_ref[...])


def _mix_in(layer, x, g, win, wconv, ab, wp, ps):
    n = SEQ // TM_IN
    hb = TM_IN // HALO
    lay = lambda i: (layer, 0, 0)
    return pl.pallas_call(
        _mix_in_kernel,
        grid=(n,),
        in_specs=[
            pl.BlockSpec((TM_IN, D_MODEL), lambda i: (i, 0)),
            pl.BlockSpec((HALO, D_MODEL), lambda i: (jnp.maximum(i * hb - 1, 0), 0)),
            pl.BlockSpec((HALO, D_MODEL), lambda i: (jnp.minimum((i + 1) * hb, SEQ // HALO - 1), 0)),
            pl.BlockSpec((None, 1, D_MODEL), lay),
            pl.BlockSpec((None, D_MODEL, IN_PROJ_DIM), lay, pipeline_mode=pl.Buffered(1)),
            pl.BlockSpec((None, 3, CONV_DIM), lay),
            pl.BlockSpec((None, FOURIER_DIM, Z_DIM), lay),
            pl.BlockSpec((None, POOL_DIM, POOL_DIM), lay),
            pl.BlockSpec((None, 1, POOL_DIM), lay),
        ],
        out_specs=[pl.BlockSpec((TM_IN, YAC_DIM), lambda i: (i, 0)),
                   pl.BlockSpec((TM_IN, Z_DIM), lambda i: (i, 0))],
        out_shape=[jax.ShapeDtypeStruct((SEQ, YAC_DIM), F32),
                   jax.ShapeDtypeStruct((SEQ, Z_DIM), F32)],
        scratch_shapes=[pltpu.VMEM((TM_IN + 2 * HALO, D_MODEL), BF16)],
        compiler_params=pltpu.CompilerParams(dimension_semantics=("arbitrary",),
                                             vmem_limit_bytes=VMEM_LIMIT),
        name="mix_in",
    )(x, x, x, g, win, wconv, ab, wp, ps)


def _dft_a_kernel(f_ref, z_ref, t_ref, slab_ref):
    h = FOURIER_DIM
    f = f_ref[...]
    for j in range(SUBLANES):
        slab_ref[...] = z_ref[:, j, :]
        res = _dot(f, slab_ref[...].astype(BF16))
        rows = slice(j * RADIX, (j + 1) * RADIX)
        t_ref[rows, 0:h] = res[0:RADIX, 0:h] - res[RADIX:2 * RADIX, h:Z_DIM]
        t_ref[rows, h:Z_DIM] = res[0:RADIX, h:Z_DIM] + res[RADIX:2 * RADIX, 0:h]


def _dft_a(f_stage1, z):
    z3 = z.reshape(RADIX, RADIX, Z_DIM)
    return pl.pallas_call(
        _dft_a_kernel,
        grid=(RADIX // SUBLANES,),
        in_specs=[pl.BlockSpec((2 * RADIX, RADIX), lambda t: (0, 0)),
                  pl.BlockSpec((RADIX, SUBLANES, Z_DIM), lambda t: (0, t, 0))],
        out_specs=pl.BlockSpec((SUBLANES * RADIX, Z_DIM), lambda t: (t, 0)),
        out_shape=jax.ShapeDtypeStruct((SEQ, Z_DIM), F32),
        scratch_shapes=[pltpu.VMEM((RADIX, Z_DIM), F32)],
        compiler_params=pltpu.CompilerParams(dimension_semantics=("arbitrary",)),
        name="dft_a",
    )(f_stage1, z3)


def _mix_out_kernel(x_ref, yac_ref, t_ref, cphi_ref, sphi_ref, cth_ref, sth_ref, wout_ref, gffn_ref,
                    w1_ref, w3_ref, w2_ref, gfin_ref, out_ref, yb_ref, ybuf, slab_ref, *, final_norm):
    chunk = pl.program_id(0)
    part = pl.program_id(1)
    c0 = CONV_DIM
    h = FOURIER_DIM

    @pl.when(part == 0)
    def _():
        cphi = cphi_ref[...]
        sphi = sphi_ref[...]
        for j in range(SUBLANES):
            c = chunk * SUBLANES + j
            cth = cth_ref[pl.ds(c, 1), :]
            sth = sth_ref[pl.ds(c, 1), :]
            mcos = (cphi * cth - sphi * sth).astype(BF16)
            msin = (sphi * cth + cphi * sth).astype(BF16)
            slab_ref[...] = t_ref[:, j, :]
            tj = slab_ref[...].astype(BF16)
            yb_ref[:, j, :] = _dot(mcos, tj[:, 0:h]) + _dot(msin, tj[:, h:Z_DIM])

    yac = yac_ref[...].reshape(TM_OUT, YAC_DIM)
    yb = yb_ref[pl.ds(pl.multiple_of(part * D_ROWS, D_ROWS), D_ROWS)].reshape(TM_OUT, h)
    ybuf[:, 0:c0] = yac[:, 0:c0].astype(BF16)
    ybuf[:, c0:c0 + h] = yb.astype(BF16)
    ybuf[:, c0 + h:D_MODEL] = yac[:, c0:YAC_DIM].astype(BF16)

    x1 = x_ref[...].reshape(TM_OUT, D_MODEL) + _dot(ybuf[...], wout_ref[...])
    xn = _rms(x1, gffn_ref[...]).astype(BF16)
    acc = None
    for k in range(D_FF // FF_CHUNK):
        cols = slice(k * FF_CHUNK, (k + 1) * FF_CHUNK)
        gate = _dot(xn, w1_ref[:, cols])
        up = _dot(xn, w3_ref[:, cols])
        hid = (gate * (1.0 / (1.0 + jnp.exp(-gate))) * up).astype(BF16)
        ffn = _dot(hid, w2_ref[cols, :])
        acc = ffn if acc is None else acc + ffn
    x2 = x1 + acc
    if final_norm:
        x2 = _rms(x2, gfin_ref[...])
    out_ref[...] = x2.reshape(D_ROWS, SUBLANES, D_MODEL)


def _mix_out(layer, x, yac, t, tabs, wout, gffn, w1, w3, w2, gfin, final_norm):
    x3 = x.reshape(RADIX, RADIX, D_MODEL)
    yac3 = yac.reshape(RADIX, RADIX, YAC_DIM)
    t3 = t.reshape(RADIX, RADIX, Z_DIM)
    const = lambda s, p: (0, 0)
    lay = lambda s, p: (layer, 0, 0)
    tile = lambda s, p: (p, s, 0)
    once = dict(pipeline_mode=pl.Buffered(1))
    out = pl.pallas_call(
        functools.partial(_mix_out_kernel, final_norm=final_norm),
        grid=(RADIX // SUBLANES, D_SPLIT),
        in_specs=[
            pl.BlockSpec((D_ROWS, SUBLANES, D_MODEL), tile),
            pl.BlockSpec((D_ROWS, SUBLANES, YAC_DIM), tile),
            pl.BlockSpec((RADIX, SUBLANES, Z_DIM), lambda s, p: (0, s, 0)),
            pl.BlockSpec((RADIX, RADIX), const),
            pl.BlockSpec((RADIX, RADIX), const),
            pl.BlockSpec((RADIX, RADIX), const),
            pl.BlockSpec((RADIX, RADIX), const),
            pl.BlockSpec((None, D_MODEL, D_MODEL), lay, **once),
            pl.BlockSpec((None, 1, D_MODEL), lay),
            pl.BlockSpec((None, D_MODEL, D_FF), lay, **once),
            pl.BlockSpec((None, D_MODEL, D_FF), lay, **once),
            pl.BlockSpec((None, D_FF, D_MODEL), lay, **once),
            pl.BlockSpec((1, D_MODEL), const),
        ],
        out_specs=pl.BlockSpec((D_ROWS, SUBLANES, D_MODEL), tile),
        out_shape=jax.ShapeDtypeStruct((RADIX, RADIX, D_MODEL), F32),
        scratch_shapes=[pltpu.VMEM((RADIX, SUBLANES, FOURIER_DIM), F32),
                        pltpu.VMEM((TM_OUT, D_MODEL), BF16),
                        pltpu.VMEM((RADIX, Z_DIM), F32)],
        compiler_params=pltpu.CompilerParams(dimension_semantics=("arbitrary", "arbitrary"),
                                             vmem_limit_bytes=VMEM_LIMIT),
        name="mix_out",
    )(x3, yac3, t3, tabs["cphi"], tabs["sphi"], tabs["cth"], tabs["sth"], wout, gffn, w1, w3, w2, gfin)
    return out.reshape(SEQ, D_MODEL)


def kernel(x, g_mix, w_in, w_conv, w_fourier, w_pool, pool_scale, w_out, g_ffn, w1, w3, w2, g_final):
    tabs = {k: jnp.asarray(v) for k, v in _tables().items()}
    block_diag = jax.scipy.linalg.block_diag
    wf_bd = jnp.stack([block_diag(*[w_fourier[l, h] for h in range(FOURIER_HEADS)]) for l in range(DEPTH)])
    wp_bd = jnp.stack([block_diag(*[w_pool[l, g] for g in range(len(POOL_WINDOWS))]) for l in range(DEPTH)])
    ab = _fold_fourier(tabs["chan"], wf_bd)
    f_stage1 = tabs["f_stage1"].astype(BF16)
    win, wout, w1b, w3b, w2b, wpb = (w.astype(BF16) for w in (w_in, w_out, w1, w3, w2, wp_bd))
    gm, gf, ps = (v.reshape(DEPTH, 1, -1) for v in (g_mix, g_ffn, pool_scale))

    xs = x.reshape(SEQ, D_MODEL)
    for l in range(DEPTH):
        yac, z = _mix_in(l, xs, gm, win, w_conv, ab, wpb, ps)
        t = _dft_a(f_stage1, z)
        xs = _mix_out(l, xs, yac, t, tabs, wout, gf, w1b, w3b, w2b, g_final[None, :],
                      final_norm=(l == DEPTH - 1))
    return xs.reshape(1, SEQ, D_MODEL)
```

```python
import functools

import numpy as np
import jax
import jax.numpy as jnp
from jax import lax
from jax.experimental import pallas as pl
from jax.experimental.pallas import tpu as pltpu

D_MODEL = 1024
SEQ = 16384
DEPTH = 2
HEAD_DIM = 64
CONV_DIM = 384
FOURIER_HEADS = 4
FOURIER_DIM = 256
POOL_WINDOWS = (2, 4, 8, 16)
POOL_GROUP_DIM = 96
POOL_DIM = 384
IN_PROJ_DIM = 3 * CONV_DIM + FOURIER_DIM + POOL_DIM
D_FF = 2816
EPS = 1e-6

RADIX = 128
YAC_DIM = CONV_DIM + POOL_DIM
Z_DIM = 2 * FOURIER_DIM
SUBLANES = 8

TM_IN = 1024
HALO = 8
D_SPLIT = 2
D_ROWS = RADIX // D_SPLIT
TM_OUT = D_ROWS * SUBLANES
MXU_DIM = 256
FF_CHUNKS = (1536, 1280)
assert sum(FF_CHUNKS) == D_FF and all(w % MXU_DIM == 0 for w in FF_CHUNKS)
VMEM_LIMIT = 56 * 1024 * 1024

F32 = jnp.float32
BF16 = jnp.bfloat16
HIGHEST = lax.Precision.HIGHEST


def _dot(a, b):
    return jnp.dot(a, b, preferred_element_type=F32)


def _rms(v, g):
    ms = jnp.mean(v * v, axis=-1, keepdims=True)
    return v * lax.rsqrt(ms + EPS) * g


def _tables():
    r = np.arange(RADIX, dtype=np.float64)
    ang128 = 2.0 * np.pi * np.outer(r, r) / RADIX
    angs = 2.0 * np.pi * np.outer(r, r) / SEQ
    f_stage1 = np.concatenate([np.cos(ang128), -np.sin(ang128)], axis=0)
    e = np.arange(HEAD_DIM, dtype=np.float64)
    angc = 2.0 * np.pi * np.outer(e, e) / HEAD_DIM
    norm = 1.0 / np.sqrt(float(SEQ) * HEAD_DIM)
    eye = np.eye(FOURIER_HEADS)
    cbd = np.kron(eye, np.cos(angc)) * norm
    sbd = np.kron(eye, -np.sin(angc)) * norm
    return dict(
        f_stage1=f_stage1.astype(np.float32),
        cphi=np.cos(ang128).astype(np.float32), sphi=np.sin(ang128).astype(np.float32),
        cth=np.cos(angs).astype(np.float32), sth=np.sin(angs).astype(np.float32),
        chan=np.stack([cbd, sbd]).astype(np.float32),
    )


WIN_CH = 0
WIN_P = 2 * CONV_DIM
WIN_B = 2 * CONV_DIM + POOL_DIM
WIN_F = 3 * CONV_DIM + POOL_DIM
WIN_DIM = WIN_F + Z_DIM
assert WIN_DIM % (2 * MXU_DIM) == 0


def _dot_exact(a, b):
    return jnp.dot(a, b, precision=HIGHEST, preferred_element_type=F32)


def _fold_kernel(chan_ref, wf_ref, wp_ref, ps_ref, wout_ref, win_ref, woe_ref, wine_ref):
    mixed = CONV_DIM + FOURIER_DIM
    woe_ref[0:mixed, :] = wout_ref[0:mixed, :].astype(BF16)
    woe_ref[mixed:D_MODEL, :] = _dot_exact(wp_ref[...] * ps_ref[...], wout_ref[mixed:D_MODEL, :]).astype(BF16)
    c0 = CONV_DIM
    wine_ref[:, WIN_CH:WIN_P] = win_ref[:, c0:3 * c0].astype(BF16)
    wine_ref[:, WIN_P:WIN_B] = win_ref[:, 3 * c0 + FOURIER_DIM:IN_PROJ_DIM].astype(BF16)
    wine_ref[:, WIN_B:WIN_F] = win_ref[:, 0:c0].astype(BF16)
    wfour = win_ref[:, 3 * c0:3 * c0 + FOURIER_DIM]
    wf = wf_ref[...]
    wine_ref[:, WIN_F:WIN_F + FOURIER_DIM] = _dot_exact(wfour, _dot_exact(chan_ref[0], wf)).astype(BF16)
    wine_ref[:, WIN_F + FOURIER_DIM:WIN_DIM] = _dot_exact(wfour, _dot_exact(chan_ref[1], wf)).astype(BF16)


def _fold_weights(chan, wf_bd, wp_bd, ps, w_out, w_in):
    lay = lambda l: (l, 0, 0)
    return pl.pallas_call(
        _fold_kernel,
        grid=(DEPTH,),
        in_specs=[pl.BlockSpec((2, FOURIER_DIM, FOURIER_DIM), lambda l: (0, 0, 0)),
                  pl.BlockSpec((None, FOURIER_DIM, FOURIER_DIM), lay),
                  pl.BlockSpec((None, POOL_DIM, POOL_DIM), lay),
                  pl.BlockSpec((None, 1, POOL_DIM), lay),
                  pl.BlockSpec((None, D_MODEL, D_MODEL), lay),
                  pl.BlockSpec((None, D_MODEL, IN_PROJ_DIM), lay)],
        out_specs=[pl.BlockSpec((None, D_MODEL, D_MODEL), lay),
                   pl.BlockSpec((None, D_MODEL, WIN_DIM), lay)],
        out_shape=[jax.ShapeDtypeStruct((DEPTH, D_MODEL, D_MODEL), BF16),
                   jax.ShapeDtypeStruct((DEPTH, D_MODEL, WIN_DIM), BF16)],
        compiler_params=pltpu.CompilerParams(dimension_semantics=("arbitrary",),
                                             vmem_limit_bytes=VMEM_LIMIT),
        name="fold_weights",
    )(chan, wf_bd, wp_bd, ps, w_out, w_in)


def _mix_in_kernel(x_ref, xprev_ref, xnext_ref, g_ref, win_ref, wconv_ref, yac_ref, z_ref, xe_ref):
    i = pl.program_id(0)
    n = pl.num_programs(0)
    rows = TM_IN + 2 * HALO
    g = g_ref[...]
    prev = jnp.where(i > 0, _rms(xprev_ref[...], g), 0.0)
    nxt = jnp.where(i < n - 1, _rms(xnext_ref[...], g), 0.0)
    xe_ref[0:TM_IN, :] = _rms(x_ref[...], g).astype(BF16)
    xe_ref[TM_IN:rows, :] = jnp.concatenate([nxt, prev], axis=0).astype(BF16)

    z = _dot(xe_ref[...], win_ref[...])
    c0 = CONV_DIM
    tile = slice(0, TM_IN)
    zp = z[:, WIN_P:WIN_B]
    z_ref[...] = z[tile, WIN_F:WIN_DIM]

    u = z[:, WIN_CH:WIN_CH + c0] * z[:, WIN_CH + c0:WIN_P]
    conv = (wconv_ref[0:1, :] * pltpu.roll(u, 1, 0) + wconv_ref[1:2, :] * u
            + wconv_ref[2:3, :] * pltpu.roll(u, rows - 1, 0))
    yac_ref[:, 0:c0] = z[tile, WIN_B:WIN_F] * conv[tile]

    s1 = zp + pltpu.roll(zp, 1, 0)
    s2 = pltpu.roll(s1, 1, 0) + pltpu.roll(s1, rows - 1, 0)
    s4 = pltpu.roll(s2, 2, 0) + pltpu.roll(s2, rows - 2, 0)
    s8 = pltpu.roll(s4, 4, 0) + pltpu.roll(s4, rows - 4, 0)
    lane = lax.broadcasted_iota(jnp.int32, (1, POOL_DIM), 1)
    g0 = lane < POOL_GROUP_DIM
    g1 = lane < 2 * POOL_GROUP_DIM
    g2 = lane < 3 * POOL_GROUP_DIM
    half = jnp.where(g0, POOL_WINDOWS[0] // 2,
                     jnp.where(g1, POOL_WINDOWS[1] // 2,
                               jnp.where(g2, POOL_WINDOWS[2] // 2, POOL_WINDOWS[3] // 2)))
    wsum = jnp.where(g0, s1[tile], jnp.where(g1, s2[tile], jnp.where(g2, s4[tile], s8[tile])))
    yac_ref[:, c0:YAC_DIM] = wsum * (1.0 / (2 * half).astype(F32)) - zp[tile]

    def clipped_window_rows(local, first_t):
        t = first_t + lax.broadcasted_iota(jnp.int32, (HALO, POOL_DIM), 0)
        cnt = (jnp.minimum(t + half, SEQ) - jnp.maximum(t - half, 0)).astype(F32)
        yac_ref[local, c0:YAC_DIM] = wsum[local] / cnt - zp[local]

    @pl.when(i == 0)
    def _():
        clipped_window_rows(slice(0, HALO), 0)

    @pl.when(i == n - 1)
    def _():
        clipped_window_rows(slice(TM_IN - HALO, TM_IN), SEQ - HALO)


def _mix_in(layer, x, g, win, wconv):
    n = SEQ // TM_IN
    hb = TM_IN // HALO
    lay = lambda i: (layer, 0, 0)
    return pl.pallas_call(
        _mix_in_kernel,
        grid=(n,),
        in_specs=[
            pl.BlockSpec((TM_IN, D_MODEL), lambda i: (i, 0)),
            pl.BlockSpec((HALO, D_MODEL), lambda i: (jnp.maximum(i * hb - 1, 0), 0)),
            pl.BlockSpec((HALO, D_MODEL), lambda i: (jnp.minimum((i + 1) * hb, SEQ // HALO - 1), 0)),
            pl.BlockSpec((None, 1, D_MODEL), lay),
            pl.BlockSpec((None, D_MODEL, WIN_DIM), lay, pipeline_mode=pl.Buffered(1)),
            pl.BlockSpec((None, 3, CONV_DIM), lay),
        ],
        out_specs=[pl.BlockSpec((TM_IN, YAC_DIM), lambda i: (i, 0)),
                   pl.BlockSpec((TM_IN, Z_DIM), lambda i: (i, 0))],
        out_shape=[jax.ShapeDtypeStruct((SEQ, YAC_DIM), F32),
                   jax.ShapeDtypeStruct((SEQ, Z_DIM), F32)],
        scratch_shapes=[pltpu.VMEM((TM_IN + 2 * HALO, D_MODEL), BF16)],
        compiler_params=pltpu.CompilerParams(dimension_semantics=("arbitrary",),
                                             vmem_limit_bytes=VMEM_LIMIT),
        name="mix_in",
    )(x, x, x, g, win, wconv)


def _dft_a_kernel(f_ref, z_ref, t_ref, slab_ref):
    h = FOURIER_DIM
    f = f_ref[...]
    for j in range(SUBLANES):
        slab_ref[...] = z_ref[:, j, :]
        res = _dot(f, slab_ref[...].astype(BF16))
        rows = slice(j * RADIX, (j + 1) * RADIX)
        t_ref[rows, 0:h] = res[0:RADIX, 0:h] - res[RADIX:2 * RADIX, h:Z_DIM]
        t_ref[rows, h:Z_DIM] = res[0:RADIX, h:Z_DIM] + res[RADIX:2 * RADIX, 0:h]


def _dft_a(f_stage1, z):
    z3 = z.reshape(RADIX, RADIX, Z_DIM)
    return pl.pallas_call(
        _dft_a_kernel,
        grid=(RADIX // SUBLANES,),
        in_specs=[pl.BlockSpec((2 * RADIX, RADIX), lambda t: (0, 0)),
                  pl.BlockSpec((RADIX, SUBLANES, Z_DIM), lambda t: (0, t, 0))],
        out_specs=pl.BlockSpec((SUBLANES * RADIX, Z_DIM), lambda t: (t, 0)),
        out_shape=jax.ShapeDtypeStruct((SEQ, Z_DIM), F32),
        scratch_shapes=[pltpu.VMEM((RADIX, Z_DIM), F32)],
        compiler_params=pltpu.CompilerParams(dimension_semantics=("arbitrary",)),
        name="dft_a",
    )(f_stage1, z3)


def _mix_out_kernel(x_ref, yac_ref, t_ref, cphi_ref, sphi_ref, cth_ref, sth_ref, wout_ref, gffn_ref,
                    w1_ref, w3_ref, w2_ref, gfin_ref, out_ref, yb_ref, ybuf, slab_ref, *, final_norm):
    chunk = pl.program_id(0)
    part = pl.program_id(1)
    c0 = CONV_DIM
    h = FOURIER_DIM

    @pl.when(part == 0)
    def _():
        cphi = cphi_ref[...]
        sphi = sphi_ref[...]
        for j in range(SUBLANES):
            c = chunk * SUBLANES + j
            cth = cth_ref[pl.ds(c, 1), :]
            sth = sth_ref[pl.ds(c, 1), :]
            mcos = (cphi * cth - sphi * sth).astype(BF16)
            msin = (sphi * cth + cphi * sth).astype(BF16)
            slab_ref[...] = t_ref[:, j, :]
            tj = slab_ref[...].astype(BF16)
            stage = _dot(jnp.concatenate([mcos, msin], axis=1),
                         jnp.concatenate([tj[:, 0:h], tj[:, h:Z_DIM]], axis=0))
            yb_ref[:, j, :] = stage

    yac = yac_ref[...].reshape(TM_OUT, YAC_DIM)
    yb = yb_ref[pl.ds(pl.multiple_of(part * D_ROWS, D_ROWS), D_ROWS)].reshape(TM_OUT, h)
    ybuf[:, 0:c0] = yac[:, 0:c0].astype(BF16)
    ybuf[:, c0:c0 + h] = yb.astype(BF16)
    ybuf[:, c0 + h:D_MODEL] = yac[:, c0:YAC_DIM].astype(BF16)

    x1 = x_ref[...].reshape(TM_OUT, D_MODEL) + _dot(ybuf[...], wout_ref[...])
    xn = _rms(x1, gffn_ref[...]).astype(BF16)
    acc = None
    start = 0
    for width in FF_CHUNKS:
        cols = slice(start, start + width)
        start += width
        gate = _dot(xn, w1_ref[:, cols])
        up = _dot(xn, w3_ref[:, cols])
        hid = (gate * (1.0 / (1.0 + jnp.exp(-gate))) * up).astype(BF16)
        ffn = _dot(hid, w2_ref[cols, :])
        acc = ffn if acc is None else acc + ffn
    x2 = x1 + acc
    if final_norm:
        x2 = _rms(x2, gfin_ref[...])
    out_ref[...] = x2.reshape(D_ROWS, SUBLANES, D_MODEL)


def _mix_out(layer, x, yac, t, tabs, wout, gffn, w1, w3, w2, gfin, final_norm):
    x3 = x.reshape(RADIX, RADIX, D_MODEL)
    yac3 = yac.reshape(RADIX, RADIX, YAC_DIM)
    t3 = t.reshape(RADIX, RADIX, Z_DIM)
    const = lambda s, p: (0, 0)
    lay = lambda s, p: (layer, 0, 0)
    tile = lambda s, p: (p, s, 0)
    once = dict(pipeline_mode=pl.Buffered(1))
    out = pl.pallas_call(
        functools.partial(_mix_out_kernel, final_norm=final_norm),
        grid=(RADIX // SUBLANES, D_SPLIT),
        in_specs=[
            pl.BlockSpec((D_ROWS, SUBLANES, D_MODEL), tile),
            pl.BlockSpec((D_ROWS, SUBLANES, YAC_DIM), tile),
            pl.BlockSpec((RADIX, SUBLANES, Z_DIM), lambda s, p: (0, s, 0)),
            pl.BlockSpec((RADIX, RADIX), const),
            pl.BlockSpec((RADIX, RADIX), const),
            pl.BlockSpec((RADIX, RADIX), const),
            pl.BlockSpec((RADIX, RADIX), const),
            pl.BlockSpec((None, D_MODEL, D_MODEL), lay, **once),
            pl.BlockSpec((None, 1, D_MODEL), lay),
            pl.BlockSpec((None, D_MODEL, D_FF), lay, **once),
            pl.BlockSpec((None, D_MODEL, D_FF), lay, **once),
            pl.BlockSpec((None, D_FF, D_MODEL), lay, **once),
            pl.BlockSpec((1, D_MODEL), const),
        ],
        out_specs=pl.BlockSpec((D_ROWS, SUBLANES, D_MODEL), tile),
        out_shape=jax.ShapeDtypeStruct((RADIX, RADIX, D_MODEL), F32),
        scratch_shapes=[pltpu.VMEM((RADIX, SUBLANES, FOURIER_DIM), F32),
                        pltpu.VMEM((TM_OUT, D_MODEL), BF16),
                        pltpu.VMEM((RADIX, Z_DIM), F32)],
        compiler_params=pltpu.CompilerParams(dimension_semantics=("arbitrary", "arbitrary"),
                                             vmem_limit_bytes=VMEM_LIMIT),
        name="mix_out",
    )(x3, yac3, t3, tabs["cphi"], tabs["sphi"], tabs["cth"], tabs["sth"], wout, gffn, w1, w3, w2, gfin)
    return out.reshape(SEQ, D_MODEL)


def kernel(x, g_mix, w_in, w_conv, w_fourier, w_pool, pool_scale, w_out, g_ffn, w1, w3, w2, g_final):
    tabs = {k: jnp.asarray(v) for k, v in _tables().items()}
    block_diag = jax.scipy.linalg.block_diag
    wf_bd = jnp.stack([block_diag(*[w_fourier[l, h] for h in range(FOURIER_HEADS)]) for l in range(DEPTH)])
    wp_bd = jnp.stack([block_diag(*[w_pool[l, g] for g in range(len(POOL_WINDOWS))]) for l in range(DEPTH)])
    gm, gf, ps = (v.reshape(DEPTH, 1, -1) for v in (g_mix, g_ffn, pool_scale))
    wout, win = _fold_weights(tabs["chan"], wf_bd, wp_bd, ps, w_out, w_in)
    f_stage1 = tabs["f_stage1"].astype(BF16)
    w1b, w3b, w2b = (w.astype(BF16) for w in (w1, w3, w2))

    xs = x.reshape(SEQ, D_MODEL)
    for l in range(DEPTH):
        yac, z = _mix_in(l, xs, gm, win, w_conv)
        t = _dft_a(f_stage1, z)
        xs = _mix_out(l, xs, yac, t, tabs, wout, gf, w1b, w3b, w2b, g_final[None, :],
                      final_norm=(l == DEPTH - 1))
    return xs.reshape(1, SEQ, D_MODEL)
```

```python
import functools

import numpy as np
import jax
import jax.numpy as jnp
from jax import lax
from jax.experimental import pallas as pl
from jax.experimental.pallas import tpu as pltpu

D_MODEL = 1024
SEQ = 16384
DEPTH = 2
HEAD_DIM = 64
CONV_DIM = 384
FOURIER_HEADS = 4
FOURIER_DIM = 256
POOL_WINDOWS = (2, 4, 8, 16)
POOL_GROUP_DIM = 96
POOL_DIM = 384
IN_PROJ_DIM = 3 * CONV_DIM + FOURIER_DIM + POOL_DIM
D_FF = 2816
EPS = 1e-6

RADIX = 128
YAC_DIM = CONV_DIM + POOL_DIM
Z_DIM = 2 * FOURIER_DIM
SUBLANES = 8

TM_IN = 1024
HALO = 8
D_SPLIT = 2
D_ROWS = RADIX // D_SPLIT
TM_OUT = D_ROWS * SUBLANES
MXU_DIM = 256
FF_CHUNKS = (1536, 1280)
assert sum(FF_CHUNKS) == D_FF and all(w % MXU_DIM == 0 for w in FF_CHUNKS)
VMEM_LIMIT = 56 * 1024 * 1024

F32 = jnp.float32
BF16 = jnp.bfloat16
HIGHEST = lax.Precision.HIGHEST


def _dot(a, b):
    return jnp.dot(a, b, preferred_element_type=F32)


def _rms(v, g):
    ms = jnp.mean(v * v, axis=-1, keepdims=True)
    return v * lax.rsqrt(ms + EPS) * g


def _tables():
    r = np.arange(RADIX, dtype=np.float64)
    ang128 = 2.0 * np.pi * np.outer(r, r) / RADIX
    angs = 2.0 * np.pi * np.outer(r, r) / SEQ
    f_stage1 = np.concatenate([np.cos(ang128), -np.sin(ang128)], axis=0)
    e = np.arange(HEAD_DIM, dtype=np.float64)
    angc = 2.0 * np.pi * np.outer(e, e) / HEAD_DIM
    norm = 1.0 / np.sqrt(float(SEQ) * HEAD_DIM)
    eye = np.eye(FOURIER_HEADS)
    cbd = np.kron(eye, np.cos(angc)) * norm
    sbd = np.kron(eye, -np.sin(angc)) * norm
    return dict(
        f_stage1=f_stage1.astype(np.float32),
        cphi=np.cos(ang128).astype(np.float32), sphi=np.sin(ang128).astype(np.float32),
        cth=np.cos(angs).astype(np.float32), sth=np.sin(angs).astype(np.float32),
        chan=np.stack([cbd, sbd]).astype(np.float32),
    )


WIN_CH = 0
WIN_P = 2 * CONV_DIM
WIN_B = 2 * CONV_DIM + POOL_DIM
WIN_F = 3 * CONV_DIM + POOL_DIM
WIN_DIM = WIN_F + Z_DIM
assert WIN_DIM % (2 * MXU_DIM) == 0


def _dot_exact(a, b):
    return jnp.dot(a, b, precision=HIGHEST, preferred_element_type=F32)


def _fold_kernel(chan_ref, wf_ref, wp_ref, ps_ref, wout_ref, win_ref, woe_ref, wine_ref):
    mixed = CONV_DIM + FOURIER_DIM
    woe_ref[0:mixed, :] = wout_ref[0:mixed, :].astype(BF16)
    woe_ref[mixed:D_MODEL, :] = _dot_exact(wp_ref[...] * ps_ref[...], wout_ref[mixed:D_MODEL, :]).astype(BF16)
    c0 = CONV_DIM
    wine_ref[:, WIN_CH:WIN_P] = win_ref[:, c0:3 * c0].astype(BF16)
    wine_ref[:, WIN_P:WIN_B] = win_ref[:, 3 * c0 + FOURIER_DIM:IN_PROJ_DIM].astype(BF16)
    wine_ref[:, WIN_B:WIN_F] = win_ref[:, 0:c0].astype(BF16)
    wfour = win_ref[:, 3 * c0:3 * c0 + FOURIER_DIM]
    wf = wf_ref[...]
    wine_ref[:, WIN_F:WIN_F + FOURIER_DIM] = _dot_exact(wfour, _dot_exact(chan_ref[0], wf)).astype(BF16)
    wine_ref[:, WIN_F + FOURIER_DIM:WIN_DIM] = _dot_exact(wfour, _dot_exact(chan_ref[1], wf)).astype(BF16)


def _fold_weights(chan, wf_bd, wp_bd, ps, w_out, w_in):
    lay = lambda l: (l, 0, 0)
    return pl.pallas_call(
        _fold_kernel,
        grid=(DEPTH,),
        in_specs=[pl.BlockSpec((2, FOURIER_DIM, FOURIER_DIM), lambda l: (0, 0, 0)),
                  pl.BlockSpec((None, FOURIER_DIM, FOURIER_DIM), lay),
                  pl.BlockSpec((None, POOL_DIM, POOL_DIM), lay),
                  pl.BlockSpec((None, 1, POOL_DIM), lay),
                  pl.BlockSpec((None, D_MODEL, D_MODEL), lay),
                  pl.BlockSpec((None, D_MODEL, IN_PROJ_DIM), lay)],
        out_specs=[pl.BlockSpec((None, D_MODEL, D_MODEL), lay),
                   pl.BlockSpec((None, D_MODEL, WIN_DIM), lay)],
        out_shape=[jax.ShapeDtypeStruct((DEPTH, D_MODEL, D_MODEL), BF16),
                   jax.ShapeDtypeStruct((DEPTH, D_MODEL, WIN_DIM), BF16)],
        compiler_params=pltpu.CompilerParams(dimension_semantics=("arbitrary",),
                                             vmem_limit_bytes=VMEM_LIMIT),
        name="fold_weights",
    )(chan, wf_bd, wp_bd, ps, w_out, w_in)


def _mix_in_kernel(x_ref, xprev_ref, xnext_ref, g_ref, win_ref, wconv_ref, w1_ref, w3_ref, w2_ref,
                   yac_ref, z_ref, w1b_ref, w3b_ref, w2b_ref, xe_ref):
    w1b_ref[...] = w1_ref[...].astype(BF16)
    w3b_ref[...] = w3_ref[...].astype(BF16)
    w2b_ref[...] = w2_ref[...].astype(BF16)

    i = pl.program_id(0)
    n = pl.num_programs(0)
    rows = TM_IN + 2 * HALO
    g = g_ref[...]
    prev = jnp.where(i > 0, _rms(xprev_ref[...], g), 0.0)
    nxt = jnp.where(i < n - 1, _rms(xnext_ref[...], g), 0.0)
    xe_ref[0:TM_IN, :] = _rms(x_ref[...], g).astype(BF16)
    xe_ref[TM_IN:rows, :] = jnp.concatenate([nxt, prev], axis=0).astype(BF16)

    z = _dot(xe_ref[...], win_ref[...])
    c0 = CONV_DIM
    tile = slice(0, TM_IN)
    zp = z[:, WIN_P:WIN_B]
    z_ref[...] = z[tile, WIN_F:WIN_DIM]

    u = z[:, WIN_CH:WIN_CH + c0] * z[:, WIN_CH + c0:WIN_P]
    conv = (wconv_ref[0:1, :] * pltpu.roll(u, 1, 0) + wconv_ref[1:2, :] * u
            + wconv_ref[2:3, :] * pltpu.roll(u, rows - 1, 0))
    yac_ref[:, 0:c0] = z[tile, WIN_B:WIN_F] * conv[tile]

    s1 = zp + pltpu.roll(zp, 1, 0)
    s2 = pltpu.roll(s1, 1, 0) + pltpu.roll(s1, rows - 1, 0)
    s4 = pltpu.roll(s2, 2, 0) + pltpu.roll(s2, rows - 2, 0)
    s8 = pltpu.roll(s4, 4, 0) + pltpu.roll(s4, rows - 4, 0)
    lane = lax.broadcasted_iota(jnp.int32, (1, POOL_DIM), 1)
    g0 = lane < POOL_GROUP_DIM
    g1 = lane < 2 * POOL_GROUP_DIM
    g2 = lane < 3 * POOL_GROUP_DIM
    half = jnp.where(g0, POOL_WINDOWS[0] // 2,
                     jnp.where(g1, POOL_WINDOWS[1] // 2,
                               jnp.where(g2, POOL_WINDOWS[2] // 2, POOL_WINDOWS[3] // 2)))
    wsum = jnp.where(g0, s1[tile], jnp.where(g1, s2[tile], jnp.where(g2, s4[tile], s8[tile])))
    yac_ref[:, c0:YAC_DIM] = wsum * (1.0 / (2 * half).astype(F32)) - zp[tile]

    def clipped_window_rows(local, first_t):
        t = first_t + lax.broadcasted_iota(jnp.int32, (HALO, POOL_DIM), 0)
        cnt = (jnp.minimum(t + half, SEQ) - jnp.maximum(t - half, 0)).astype(F32)
        yac_ref[local, c0:YAC_DIM] = wsum[local] / cnt - zp[local]

    @pl.when(i == 0)
    def _():
        clipped_window_rows(slice(0, HALO), 0)

    @pl.when(i == n - 1)
    def _():
        clipped_window_rows(slice(TM_IN - HALO, TM_IN), SEQ - HALO)


def _mix_in(layer, x, g, win, wconv, w1, w3, w2):
    n = SEQ // TM_IN
    hb = TM_IN // HALO
    lay = lambda i: (layer, 0, 0)
    w_rows = lambda i: (layer, i, 0)
    cast_rows = lambda i: (i, 0)
    return pl.pallas_call(
        _mix_in_kernel,
        grid=(n,),
        in_specs=[
            pl.BlockSpec((TM_IN, D_MODEL), lambda i: (i, 0)),
            pl.BlockSpec((HALO, D_MODEL), lambda i: (jnp.maximum(i * hb - 1, 0), 0)),
            pl.BlockSpec((HALO, D_MODEL), lambda i: (jnp.minimum((i + 1) * hb, SEQ // HALO - 1), 0)),
            pl.BlockSpec((None, 1, D_MODEL), lay),
            pl.BlockSpec((None, D_MODEL, WIN_DIM), lay, pipeline_mode=pl.Buffered(1)),
            pl.BlockSpec((None, 3, CONV_DIM), lay),
            pl.BlockSpec((None, D_MODEL // n, D_FF), w_rows),
            pl.BlockSpec((None, D_MODEL // n, D_FF), w_rows),
            pl.BlockSpec((None, D_FF // n, D_MODEL), w_rows),
        ],
        out_specs=[pl.BlockSpec((TM_IN, YAC_DIM), lambda i: (i, 0)),
                   pl.BlockSpec((TM_IN, Z_DIM), lambda i: (i, 0)),
                   pl.BlockSpec((D_MODEL // n, D_FF), cast_rows),
                   pl.BlockSpec((D_MODEL // n, D_FF), cast_rows),
                   pl.BlockSpec((D_FF // n, D_MODEL), cast_rows)],
        out_shape=[jax.ShapeDtypeStruct((SEQ, YAC_DIM), F32),
                   jax.ShapeDtypeStruct((SEQ, Z_DIM), F32),
                   jax.ShapeDtypeStruct((D_MODEL, D_FF), BF16),
                   jax.ShapeDtypeStruct((D_MODEL, D_FF), BF16),
                   jax.ShapeDtypeStruct((D_FF, D_MODEL), BF16)],
        scratch_shapes=[pltpu.VMEM((TM_IN + 2 * HALO, D_MODEL), BF16)],
        compiler_params=pltpu.CompilerParams(dimension_semantics=("arbitrary",),
                                             vmem_limit_bytes=VMEM_LIMIT),
        name="mix_in",
    )(x, x, x, g, win, wconv, w1, w3, w2)


def _dft_a_kernel(f_ref, z_ref, t_ref, slab_ref):
    h = FOURIER_DIM
    f = f_ref[...]
    for j in range(SUBLANES):
        slab_ref[...] = z_ref[:, j, :]
        res = _dot(f, slab_ref[...].astype(BF16))
        t_ref[:, j, 0:h] = res[0:RADIX, 0:h] - res[RADIX:2 * RADIX, h:Z_DIM]
        t_ref[:, j, h:Z_DIM] = res[0:RADIX, h:Z_DIM] + res[RADIX:2 * RADIX, 0:h]


def _dft_a(f_stage1, z):
    z3 = z.reshape(RADIX, RADIX, Z_DIM)
    return pl.pallas_call(
        _dft_a_kernel,
        grid=(RADIX // SUBLANES,),
        in_specs=[pl.BlockSpec((2 * RADIX, RADIX), lambda t: (0, 0)),
                  pl.BlockSpec((RADIX, SUBLANES, Z_DIM), lambda t: (0, t, 0))],
        out_specs=pl.BlockSpec((RADIX, SUBLANES, Z_DIM), lambda t: (0, t, 0)),
        out_shape=jax.ShapeDtypeStruct((RADIX, RADIX, Z_DIM), F32),
        scratch_shapes=[pltpu.VMEM((RADIX, Z_DIM), F32)],
        compiler_params=pltpu.CompilerParams(dimension_semantics=("arbitrary",)),
        name="dft_a",
    )(f_stage1, z3)


def _mix_out_kernel(x_ref, yac_ref, t_ref, cphi_ref, sphi_ref, cth_ref, sth_ref, wout_ref, gffn_ref,
                    w1_ref, w3_ref, w2_ref, gfin_ref, out_ref, yb_ref, ybuf, *, final_norm):
    chunk = pl.program_id(0)
    part = pl.program_id(1)
    c0 = CONV_DIM
    h = FOURIER_DIM

    d_rows = pl.ds(pl.multiple_of(part * D_ROWS, D_ROWS), D_ROWS)
    cphi = cphi_ref[d_rows, :]
    sphi = sphi_ref[d_rows, :]
    for j in range(SUBLANES):
        c = chunk * SUBLANES + j
        cth = cth_ref[pl.ds(c, 1), :]
        sth = sth_ref[pl.ds(c, 1), :]
        mcos = (cphi * cth - sphi * sth).astype(BF16)
        msin = (sphi * cth + cphi * sth).astype(BF16)
        tj = t_ref[j].astype(BF16)
        yb_ref[:, j, :] = _dot(jnp.concatenate([mcos, msin], axis=1),
                               jnp.concatenate([tj[:, 0:h], tj[:, h:Z_DIM]], axis=0))

    yac = yac_ref[...].reshape(TM_OUT, YAC_DIM)
    yb = yb_ref[...].reshape(TM_OUT, h)
    ybuf[:, 0:c0] = yac[:, 0:c0].astype(BF16)
    ybuf[:, c0:c0 + h] = yb.astype(BF16)
    ybuf[:, c0 + h:D_MODEL] = yac[:, c0:YAC_DIM].astype(BF16)

    x1 = x_ref[...].reshape(TM_OUT, D_MODEL) + _dot(ybuf[...], wout_ref[...])
    xn = _rms(x1, gffn_ref[...]).astype(BF16)
    acc = None
    start = 0
    for width in FF_CHUNKS:
        cols = slice(start, start + width)
        start += width
        gate = _dot(xn, w1_ref[:, cols])
        up = _dot(xn, w3_ref[:, cols])
        hid = (gate * (1.0 / (1.0 + jnp.exp(-gate))) * up).astype(BF16)
        ffn = _dot(hid, w2_ref[cols, :])
        acc = ffn if acc is None else acc + ffn
    x2 = x1 + acc
    if final_norm:
        x2 = _rms(x2, gfin_ref[...])
    out_ref[...] = x2.reshape(D_ROWS, SUBLANES, D_MODEL)


def _mix_out(layer, x, yac, t, tabs, wout, gffn, w1, w3, w2, gfin, final_norm):
    x3 = x.reshape(RADIX, RADIX, D_MODEL)
    yac3 = yac.reshape(RADIX, RADIX, YAC_DIM)
    const = lambda s, p: (0, 0)
    lay = lambda s, p: (layer, 0, 0)
    tile = lambda s, p: (p, s, 0)
    once = dict(pipeline_mode=pl.Buffered(1))
    out = pl.pallas_call(
        functools.partial(_mix_out_kernel, final_norm=final_norm),
        grid=(RADIX // SUBLANES, D_SPLIT),
        in_specs=[
            pl.BlockSpec((D_ROWS, SUBLANES, D_MODEL), tile),
            pl.BlockSpec((D_ROWS, SUBLANES, YAC_DIM), tile),
            pl.BlockSpec((SUBLANES, RADIX, Z_DIM), lambda s, p: (s, 0, 0)),
            pl.BlockSpec((RADIX, RADIX), const),
            pl.BlockSpec((RADIX, RADIX), const),
            pl.BlockSpec((RADIX, RADIX), const),
            pl.BlockSpec((RADIX, RADIX), const),
            pl.BlockSpec((None, D_MODEL, D_MODEL), lay, **once),
            pl.BlockSpec((None, 1, D_MODEL), lay),
            pl.BlockSpec((D_MODEL, D_FF), const, **once),
            pl.BlockSpec((D_MODEL, D_FF), const, **once),
            pl.BlockSpec((D_FF, D_MODEL), const, **once),
            pl.BlockSpec((1, D_MODEL), const),
        ],
        out_specs=pl.BlockSpec((D_ROWS, SUBLANES, D_MODEL), tile),
        out_shape=jax.ShapeDtypeStruct((RADIX, RADIX, D_MODEL), F32),
        scratch_shapes=[pltpu.VMEM((D_ROWS, SUBLANES, FOURIER_DIM), F32),
                        pltpu.VMEM((TM_OUT, D_MODEL), BF16)],
        compiler_params=pltpu.CompilerParams(dimension_semantics=("arbitrary", "arbitrary"),
                                             vmem_limit_bytes=VMEM_LIMIT),
        name="mix_out",
    )(x3, yac3, t, tabs["cphi"], tabs["sphi"], tabs["cth"], tabs["sth"], wout, gffn, w1, w3, w2, gfin)
    return out.reshape(SEQ, D_MODEL)


def kernel(x, g_mix, w_in, w_conv, w_fourier, w_pool, pool_scale, w_out, g_ffn, w1, w3, w2, g_final):
    tabs = {k: jnp.asarray(v) for k, v in _tables().items()}
    block_diag = jax.scipy.linalg.block_diag
    wf_bd = jnp.stack([block_diag(*[w_fourier[l, h] for h in range(FOURIER_HEADS)]) for l in range(DEPTH)])
    wp_bd = jnp.stack([block_diag(*[w_pool[l, g] for g in range(len(POOL_WINDOWS))]) for l in range(DEPTH)])
    gm, gf, ps = (v.reshape(DEPTH, 1, -1) for v in (g_mix, g_ffn, pool_scale))
    wout, win = _fold_weights(tabs["chan"], wf_bd, wp_bd, ps, w_out, w_in)
    f_stage1 = tabs["f_stage1"].astype(BF16)

    xs = x.reshape(SEQ, D_MODEL)
    for l in range(DEPTH):
        yac, z, w1b, w3b, w2b = _mix_in(l, xs, gm, win, w_conv, w1, w3, w2)
        t = _dft_a(f_stage1, z)
        xs = _mix_out(l, xs, yac, t, tabs, wout, gf, w1b, w3b, w2b, g_final[None, :],
                      final_norm=(l == DEPTH - 1))
    return xs.reshape(1, SEQ, D_MODEL)
```

```python
import functools

import numpy as np
import jax
import jax.numpy as jnp
from jax import lax
from jax.experimental import pallas as pl
from jax.experimental.pallas import tpu as pltpu

D_MODEL = 1024
SEQ = 16384
DEPTH = 2
HEAD_DIM = 64
CONV_DIM = 384
FOURIER_HEADS = 4
FOURIER_DIM = 256
POOL_WINDOWS = (2, 4, 8, 16)
POOL_GROUP_DIM = 96
POOL_DIM = 384
IN_PROJ_DIM = 3 * CONV_DIM + FOURIER_DIM + POOL_DIM
D_FF = 2816
EPS = 1e-6

RADIX = 128
YAC_DIM = CONV_DIM + POOL_DIM
Z_DIM = 2 * FOURIER_DIM
SUBLANES = 8
DFT_COLS = 16

TM_IN = 1024
HALO = 8
D_SPLIT = 2
D_ROWS = RADIX // D_SPLIT
TM_OUT = D_ROWS * SUBLANES
MXU_DIM = 256
FF_CHUNKS = (1536, 1280)
assert sum(FF_CHUNKS) == D_FF and all(w % MXU_DIM == 0 for w in FF_CHUNKS)
VMEM_LIMIT = 56 * 1024 * 1024

F32 = jnp.float32
BF16 = jnp.bfloat16


def _dot(a, b):
    return jnp.dot(a, b, preferred_element_type=F32)


def _rms(v, g):
    ms = jnp.mean(v * v, axis=-1, keepdims=True)
    return v * lax.rsqrt(ms + EPS) * g


def _tables():
    r = np.arange(RADIX, dtype=np.float64)
    ang128 = 2.0 * np.pi * np.outer(r, r) / RADIX
    angs = 2.0 * np.pi * np.outer(r, r) / SEQ
    f_stage1 = np.concatenate([np.cos(ang128), -np.sin(ang128)], axis=0)
    e = np.arange(HEAD_DIM, dtype=np.float64)
    angc = 2.0 * np.pi * np.outer(e, e) / HEAD_DIM
    norm = 1.0 / np.sqrt(float(SEQ) * HEAD_DIM)
    eye = np.eye(FOURIER_HEADS)
    cbd = np.kron(eye, np.cos(angc)) * norm
    sbd = np.kron(eye, -np.sin(angc)) * norm
    return dict(
        f_stage1=f_stage1.astype(np.float32),
        cphi=np.cos(ang128).astype(np.float32), sphi=np.sin(ang128).astype(np.float32),
        cth=np.cos(angs).astype(np.float32), sth=np.sin(angs).astype(np.float32),
        chan=np.stack([cbd, sbd]).astype(np.float32),
    )


WIN_CH = 0
WIN_P = 2 * CONV_DIM
WIN_B = 2 * CONV_DIM + POOL_DIM
WIN_F = 3 * CONV_DIM + POOL_DIM
WIN_DIM = WIN_F + Z_DIM
assert WIN_DIM % (2 * MXU_DIM) == 0


def _dot_3pass(a, b):
    a_hi = a.astype(BF16)
    b_hi = b.astype(BF16)
    a_lo = (a - a_hi.astype(F32)).astype(BF16)
    b_lo = (b - b_hi.astype(F32)).astype(BF16)
    return _dot(a_hi, b_hi) + (_dot(a_hi, b_lo) + _dot(a_lo, b_hi))


def _fold_kernel(chan_ref, wf_ref, wp_ref, ps_ref, wout_ref, win_ref, woe_ref, wine_ref):
    mixed = CONV_DIM + FOURIER_DIM
    woe_ref[0:mixed, :] = wout_ref[0:mixed, :].astype(BF16)
    woe_ref[mixed:D_MODEL, :] = _dot_3pass(wp_ref[...] * ps_ref[...], wout_ref[mixed:D_MODEL, :]).astype(BF16)
    c0 = CONV_DIM
    wine_ref[:, WIN_CH:WIN_P] = win_ref[:, c0:3 * c0].astype(BF16)
    wine_ref[:, WIN_P:WIN_B] = win_ref[:, 3 * c0 + FOURIER_DIM:IN_PROJ_DIM].astype(BF16)
    wine_ref[:, WIN_B:WIN_F] = win_ref[:, 0:c0].astype(BF16)
    wfour = win_ref[:, 3 * c0:3 * c0 + FOURIER_DIM]
    wf = wf_ref[...]
    wine_ref[:, WIN_F:WIN_F + FOURIER_DIM] = _dot_3pass(wfour, _dot_3pass(chan_ref[0], wf)).astype(BF16)
    wine_ref[:, WIN_F + FOURIER_DIM:WIN_DIM] = _dot_3pass(wfour, _dot_3pass(chan_ref[1], wf)).astype(BF16)


def _fold_weights(chan, wf_bd, wp_bd, ps, w_out, w_in):
    lay = lambda l: (l, 0, 0)
    return pl.pallas_call(
        _fold_kernel,
        grid=(DEPTH,),
        in_specs=[pl.BlockSpec((2, FOURIER_DIM, FOURIER_DIM), lambda l: (0, 0, 0)),
                  pl.BlockSpec((None, FOURIER_DIM, FOURIER_DIM), lay),
                  pl.BlockSpec((None, POOL_DIM, POOL_DIM), lay),
                  pl.BlockSpec((None, 1, POOL_DIM), lay),
                  pl.BlockSpec((None, D_MODEL, D_MODEL), lay),
                  pl.BlockSpec((None, D_MODEL, IN_PROJ_DIM), lay)],
        out_specs=[pl.BlockSpec((None, D_MODEL, D_MODEL), lay),
                   pl.BlockSpec((None, D_MODEL, WIN_DIM), lay)],
        out_shape=[jax.ShapeDtypeStruct((DEPTH, D_MODEL, D_MODEL), BF16),
                   jax.ShapeDtypeStruct((DEPTH, D_MODEL, WIN_DIM), BF16)],
        compiler_params=pltpu.CompilerParams(dimension_semantics=("arbitrary",),
                                             vmem_limit_bytes=VMEM_LIMIT),
        name="fold_weights",
    )(chan, wf_bd, wp_bd, ps, w_out, w_in)


def _mix_in_kernel(x_ref, xprev_ref, xnext_ref, g_ref, win_ref, wconv_ref, w1_ref, w3_ref, w2_ref,
                   yac_ref, z_ref, w1b_ref, w3b_ref, w2b_ref, xe_ref):
    w1b_ref[...] = w1_ref[...].astype(BF16)
    w3b_ref[...] = w3_ref[...].astype(BF16)
    w2b_ref[...] = w2_ref[...].astype(BF16)

    i = pl.program_id(0)
    n = pl.num_programs(0)
    rows = TM_IN + 2 * HALO
    g = g_ref[...]
    prev = jnp.where(i > 0, _rms(xprev_ref[...], g), 0.0)
    nxt = jnp.where(i < n - 1, _rms(xnext_ref[...], g), 0.0)
    xe_ref[0:TM_IN, :] = _rms(x_ref[...], g).astype(BF16)
    xe_ref[TM_IN:rows, :] = jnp.concatenate([nxt, prev], axis=0).astype(BF16)

    z = _dot(xe_ref[...], win_ref[...])
    c0 = CONV_DIM
    tile = slice(0, TM_IN)
    zp = z[:, WIN_P:WIN_B]
    z_ref[...] = z[tile, WIN_F:WIN_DIM].astype(BF16)

    u = z[:, WIN_CH:WIN_CH + c0] * z[:, WIN_CH + c0:WIN_P]
    conv = (wconv_ref[0:1, :] * pltpu.roll(u, 1, 0) + wconv_ref[1:2, :] * u
            + wconv_ref[2:3, :] * pltpu.roll(u, rows - 1, 0))
    yac_ref[:, 0:c0] = z[tile, WIN_B:WIN_F] * conv[tile]

    s1 = zp + pltpu.roll(zp, 1, 0)
    s2 = pltpu.roll(s1, 1, 0) + pltpu.roll(s1, rows - 1, 0)
    s4 = pltpu.roll(s2, 2, 0) + pltpu.roll(s2, rows - 2, 0)
    s8 = pltpu.roll(s4, 4, 0) + pltpu.roll(s4, rows - 4, 0)
    lane = lax.broadcasted_iota(jnp.int32, (1, POOL_DIM), 1)
    g0 = lane < POOL_GROUP_DIM
    g1 = lane < 2 * POOL_GROUP_DIM
    g2 = lane < 3 * POOL_GROUP_DIM
    half = jnp.where(g0, POOL_WINDOWS[0] // 2,
                     jnp.where(g1, POOL_WINDOWS[1] // 2,
                               jnp.where(g2, POOL_WINDOWS[2] // 2, POOL_WINDOWS[3] // 2)))
    wsum = jnp.where(g0, s1[tile], jnp.where(g1, s2[tile], jnp.where(g2, s4[tile], s8[tile])))
    yac_ref[:, c0:YAC_DIM] = wsum * (1.0 / (2 * half).astype(F32)) - zp[tile]

    def clipped_window_rows(local, first_t):
        t = first_t + lax.broadcasted_iota(jnp.int32, (HALO, POOL_DIM), 0)
        cnt = (jnp.minimum(t + half, SEQ) - jnp.maximum(t - half, 0)).astype(F32)
        yac_ref[local, c0:YAC_DIM] = wsum[local] / cnt - zp[local]

    @pl.when(i == 0)
    def _():
        clipped_window_rows(slice(0, HALO), 0)

    @pl.when(i == n - 1)
    def _():
        clipped_window_rows(slice(TM_IN - HALO, TM_IN), SEQ - HALO)


def _mix_in(layer, x, g, win, wconv, w1, w3, w2):
    n = SEQ // TM_IN
    hb = TM_IN // HALO
    lay = lambda i: (layer, 0, 0)
    w_rows = lambda i: (layer, i, 0)
    cast_rows = lambda i: (i, 0)
    return pl.pallas_call(
        _mix_in_kernel,
        grid=(n,),
        in_specs=[
            pl.BlockSpec((TM_IN, D_MODEL), lambda i: (i, 0)),
            pl.BlockSpec((HALO, D_MODEL), lambda i: (jnp.maximum(i * hb - 1, 0), 0)),
            pl.BlockSpec((HALO, D_MODEL), lambda i: (jnp.minimum((i + 1) * hb, SEQ // HALO - 1), 0)),
            pl.BlockSpec((None, 1, D_MODEL), lay),
            pl.BlockSpec((None, D_MODEL, WIN_DIM), lay, pipeline_mode=pl.Buffered(1)),
            pl.BlockSpec((None, 3, CONV_DIM), lay),
            pl.BlockSpec((None, D_MODEL // n, D_FF), w_rows),
            pl.BlockSpec((None, D_MODEL // n, D_FF), w_rows),
            pl.BlockSpec((None, D_FF // n, D_MODEL), w_rows),
        ],
        out_specs=[pl.BlockSpec((TM_IN, YAC_DIM), lambda i: (i, 0)),
                   pl.BlockSpec((TM_IN, Z_DIM), lambda i: (i, 0)),
                   pl.BlockSpec((D_MODEL // n, D_FF), cast_rows),
                   pl.BlockSpec((D_MODEL // n, D_FF), cast_rows),
                   pl.BlockSpec((D_FF // n, D_MODEL), cast_rows)],
        out_shape=[jax.ShapeDtypeStruct((SEQ, YAC_DIM), F32),
                   jax.ShapeDtypeStruct((SEQ, Z_DIM), BF16),
                   jax.ShapeDtypeStruct((D_MODEL, D_FF), BF16),
                   jax.ShapeDtypeStruct((D_MODEL, D_FF), BF16),
                   jax.ShapeDtypeStruct((D_FF, D_MODEL), BF16)],
        scratch_shapes=[pltpu.VMEM((TM_IN + 2 * HALO, D_MODEL), BF16)],
        compiler_params=pltpu.CompilerParams(dimension_semantics=("arbitrary",),
                                             vmem_limit_bytes=VMEM_LIMIT),
        name="mix_in",
    )(x, x, x, g, win, wconv, w1, w3, w2)


def _dft_a_kernel(f_ref, z_ref, t_ref, zwide_ref, twide_ref, slab_ref):
    h = FOURIER_DIM
    f = f_ref[...]
    zwide_ref[...] = z_ref[...].astype(F32)
    for j in range(DFT_COLS):
        slab_ref[...] = zwide_ref[:, j, :]
        res = _dot(f, slab_ref[...].astype(BF16))
        twide_ref[:, j, 0:h] = res[0:RADIX, 0:h] - res[RADIX:2 * RADIX, h:Z_DIM]
        twide_ref[:, j, h:Z_DIM] = res[0:RADIX, h:Z_DIM] + res[RADIX:2 * RADIX, 0:h]
    t_ref[...] = twide_ref[...].astype(BF16)


def _dft_a(f_stage1, z):
    z3 = z.reshape(RADIX, RADIX, Z_DIM)
    block = (RADIX, DFT_COLS, Z_DIM)
    return pl.pallas_call(
        _dft_a_kernel,
        grid=(RADIX // DFT_COLS,),
        in_specs=[pl.BlockSpec((2 * RADIX, RADIX), lambda t: (0, 0)),
                  pl.BlockSpec(block, lambda t: (0, t, 0))],
        out_specs=pl.BlockSpec(block, lambda t: (0, t, 0)),
        out_shape=jax.ShapeDtypeStruct((RADIX, RADIX, Z_DIM), BF16),
        scratch_shapes=[pltpu.VMEM(block, F32), pltpu.VMEM(block, F32), pltpu.VMEM((RADIX, Z_DIM), F32)],
        compiler_params=pltpu.CompilerParams(dimension_semantics=("arbitrary",)),
        name="dft_a",
    )(f_stage1, z3)


def _mix_out_kernel(x_ref, yac_ref, t_ref, cphi_ref, sphi_ref, cth_ref, sth_ref, wout_ref, gffn_ref,
                    w1_ref, w3_ref, w2_ref, gfin_ref, out_ref, yb_ref, ybuf, *, final_norm):
    chunk = pl.program_id(0)
    part = pl.program_id(1)
    c0 = CONV_DIM
    h = FOURIER_DIM

    d_rows = pl.ds(pl.multiple_of(part * D_ROWS, D_ROWS), D_ROWS)
    cphi = cphi_ref[d_rows, :]
    sphi = sphi_ref[d_rows, :]
    for j in range(SUBLANES):
        c = chunk * SUBLANES + j
        cth = cth_ref[pl.ds(c, 1), :]
        sth = sth_ref[pl.ds(c, 1), :]
        mcos = (cphi * cth - sphi * sth).astype(BF16)
        msin = (sphi * cth + cphi * sth).astype(BF16)
        tj = t_ref[j]
        yb_ref[:, j, :] = _dot(jnp.concatenate([mcos, msin], axis=1),
                               jnp.concatenate([tj[:, 0:h], tj[:, h:Z_DIM]], axis=0))

    yac = yac_ref[...].reshape(TM_OUT, YAC_DIM)
    yb = yb_ref[...].reshape(TM_OUT, h)
    ybuf[:, 0:c0] = yac[:, 0:c0].astype(BF16)
    ybuf[:, c0:c0 + h] = yb.astype(BF16)
    ybuf[:, c0 + h:D_MODEL] = yac[:, c0:YAC_DIM].astype(BF16)

    x1 = x_ref[...].reshape(TM_OUT, D_MODEL) + _dot(ybuf[...], wout_ref[...])
    xg = (x1 * gffn_ref[...]).astype(BF16)
    r = lax.rsqrt(jnp.mean(x1 * x1, axis=-1, keepdims=True) + EPS)
    acc = None
    start = 0
    for width in FF_CHUNKS:
        cols = slice(start, start + width)
        start += width
        gate = _dot(xg, w1_ref[:, cols]) * r
        up = _dot(xg, w3_ref[:, cols])
        hid = (gate * (1.0 / (1.0 + jnp.exp(-gate))) * up).astype(BF16)
        ffn = _dot(hid, w2_ref[cols, :])
        acc = ffn if acc is None else acc + ffn
    x2 = x1 + acc * r
    if final_norm:
        x2 = _rms(x2, gfin_ref[...])
    out_ref[...] = x2.reshape(D_ROWS, SUBLANES, D_MODEL)


def _mix_out(layer, x, yac, t, tabs, wout, gffn, w1, w3, w2, gfin, final_norm):
    x3 = x.reshape(RADIX, RADIX, D_MODEL)
    yac3 = yac.reshape(RADIX, RADIX, YAC_DIM)
    const = lambda s, p: (0, 0)
    lay = lambda s, p: (layer, 0, 0)
    tile = lambda s, p: (p, s, 0)
    once = dict(pipeline_mode=pl.Buffered(1))
    out = pl.pallas_call(
        functools.partial(_mix_out_kernel, final_norm=final_norm),
        grid=(RADIX // SUBLANES, D_SPLIT),
        in_specs=[
            pl.BlockSpec((D_ROWS, SUBLANES, D_MODEL), tile),
            pl.BlockSpec((D_ROWS, SUBLANES, YAC_DIM), tile),
            pl.BlockSpec((SUBLANES, RADIX, Z_DIM), lambda s, p: (s, 0, 0)),
            pl.BlockSpec((RADIX, RADIX), const),
            pl.BlockSpec((RADIX, RADIX), const),
            pl.BlockSpec((RADIX, RADIX), const),
            pl.BlockSpec((RADIX, RADIX), const),
            pl.BlockSpec((None, D_MODEL, D_MODEL), lay, **once),
            pl.BlockSpec((None, 1, D_MODEL), lay),
            pl.BlockSpec((D_MODEL, D_FF), const, **once),
            pl.BlockSpec((D_MODEL, D_FF), const, **once),
            pl.BlockSpec((D_FF, D_MODEL), const, **once),
            pl.BlockSpec((1, D_MODEL), const),
        ],
        out_specs=pl.BlockSpec((D_ROWS, SUBLANES, D_MODEL), tile),
        out_shape=jax.ShapeDtypeStruct((RADIX, RADIX, D_MODEL), F32),
        scratch_shapes=[pltpu.VMEM((D_ROWS, SUBLANES, FOURIER_DIM), F32),
                        pltpu.VMEM((TM_OUT, D_MODEL), BF16)],
        compiler_params=pltpu.CompilerParams(dimension_semantics=("arbitrary", "arbitrary"),
                                             vmem_limit_bytes=VMEM_LIMIT),
        name="mix_out",
    )(x3, yac3, t, tabs["cphi"], tabs["sphi"], tabs["cth"], tabs["sth"], wout, gffn, w1, w3, w2, gfin)
    return out.reshape(SEQ, D_MODEL)


def kernel(x, g_mix, w_in, w_conv, w_fourier, w_pool, pool_scale, w_out, g_ffn, w1, w3, w2, g_final):
    tabs = {k: jnp.asarray(v) for k, v in _tables().items()}
    block_diag = jax.scipy.linalg.block_diag
    wf_bd = jnp.stack([block_diag(*[w_fourier[l, h] for h in range(FOURIER_HEADS)]) for l in range(DEPTH)])
    wp_bd = jnp.stack([block_diag(*[w_pool[l, g] for g in range(len(POOL_WINDOWS))]) for l in range(DEPTH)])
    gm, gf, ps = (v.reshape(DEPTH, 1, -1) for v in (g_mix, g_ffn, pool_scale))
    wout, win = _fold_weights(tabs["chan"], wf_bd, wp_bd, ps, w_out, w_in)
    f_stage1 = tabs["f_stage1"].astype(BF16)

    xs = x.reshape(SEQ, D_MODEL)
    for l in range(DEPTH):
        yac, z, w1b, w3b, w2b = _mix_in(l, xs, gm, win, w_conv, w1, w3, w2)
        t = _dft_a(f_stage1, z)
        xs = _mix_out(l, xs, yac, t, tabs, wout, gf, w1b, w3b, w2b, g_final[None, :],
                      final_norm=(l == DEPTH - 1))
    return xs.reshape(1, SEQ, D_MODEL)
```

```python
import functools

import numpy as np
import jax
import jax.numpy as jnp
from jax import lax
from jax.experimental import pallas as pl
from jax.experimental.pallas import tpu as pltpu

D_MODEL = 1024
SEQ = 16384
DEPTH = 2
HEAD_DIM = 64
CONV_DIM = 384
FOURIER_HEADS = 4
FOURIER_DIM = 256
POOL_WINDOWS = (2, 4, 8, 16)
POOL_GROUP_DIM = 96
POOL_DIM = 384
IN_PROJ_DIM = 3 * CONV_DIM + FOURIER_DIM + POOL_DIM
D_FF = 2816
EPS = 1e-6

RADIX = 128
YAC_DIM = CONV_DIM + POOL_DIM
Z_DIM = 2 * FOURIER_DIM
LANES = 128
SUBLANES = 8
DFT_COLS = 16

TM_IN = 1024
IN_SUBTILES = 1
IN_SUB = TM_IN // IN_SUBTILES
HALO = 8
D_SPLIT = 2
D_ROWS = RADIX // D_SPLIT
TM_OUT = D_ROWS * SUBLANES
MXU_DIM = 256
FF_CHUNKS = (1536, 1280)
assert sum(FF_CHUNKS) == D_FF and all(w % MXU_DIM == 0 for w in FF_CHUNKS)
VMEM_LIMIT = 56 * 1024 * 1024

F32 = jnp.float32
BF16 = jnp.bfloat16


def _dot(a, b):
    return jnp.dot(a, b, preferred_element_type=F32)


def _unit_rms(v):
    ms = jnp.mean(v * v, axis=-1, keepdims=True)
    return v * lax.rsqrt(ms + EPS)


def _tables():
    r = np.arange(RADIX, dtype=np.float64)
    ang128 = 2.0 * np.pi * np.outer(r, r) / RADIX
    angs = 2.0 * np.pi * np.outer(r, r) / SEQ
    f_stage1 = np.concatenate([np.cos(ang128), -np.sin(ang128)], axis=0)
    e = np.arange(HEAD_DIM, dtype=np.float64)
    angc = 2.0 * np.pi * np.outer(e, e) / HEAD_DIM
    norm = 1.0 / np.sqrt(float(SEQ) * HEAD_DIM)
    eye = np.eye(FOURIER_HEADS)
    cbd = np.kron(eye, np.cos(angc)) * norm
    sbd = np.kron(eye, -np.sin(angc)) * norm
    return dict(
        f_stage1=f_stage1.astype(np.float32),
        cphi=np.cos(ang128).astype(np.float32), sphi=np.sin(ang128).astype(np.float32),
        cth=np.cos(angs).astype(np.float32), sth=np.sin(angs).astype(np.float32),
        chan=np.stack([cbd, sbd]).astype(np.float32),
    )


WIN_CH = 0
WIN_P = 2 * CONV_DIM
WIN_B = 2 * CONV_DIM + POOL_DIM
WIN_F = 3 * CONV_DIM + POOL_DIM
WIN_DIM = WIN_F + Z_DIM
assert WIN_DIM % (2 * MXU_DIM) == 0


def _dot_3pass(a, b):
    a_hi = a.astype(BF16)
    b_hi = b.astype(BF16)
    a_lo = (a - a_hi.astype(F32)).astype(BF16)
    b_lo = (b - b_hi.astype(F32)).astype(BF16)
    return _dot(a_hi, b_hi) + (_dot(a_hi, b_lo) + _dot(a_lo, b_hi))


def _fold_kernel(chan_ref, wf_ref, wp_ref, ps_ref, wout_ref, win_ref, gmix_ref, woe_ref, wine_ref):
    mixed = CONV_DIM + FOURIER_DIM
    woe_ref[0:mixed, :] = wout_ref[0:mixed, :].astype(BF16)
    woe_ref[mixed:D_MODEL, :] = _dot_3pass(wp_ref[...] * ps_ref[...], wout_ref[mixed:D_MODEL, :]).astype(BF16)
    c0 = CONV_DIM
    gain = gmix_ref[...]
    wine_ref[:, WIN_CH:WIN_P] = (win_ref[:, c0:3 * c0] * gain).astype(BF16)
    wine_ref[:, WIN_P:WIN_B] = (win_ref[:, 3 * c0 + FOURIER_DIM:IN_PROJ_DIM] * gain).astype(BF16)
    wine_ref[:, WIN_B:WIN_F] = (win_ref[:, 0:c0] * gain).astype(BF16)
    wfour = win_ref[:, 3 * c0:3 * c0 + FOURIER_DIM] * gain
    wf = wf_ref[...]
    wine_ref[:, WIN_F:WIN_F + FOURIER_DIM] = _dot_3pass(wfour, _dot_3pass(chan_ref[0], wf)).astype(BF16)
    wine_ref[:, WIN_F + FOURIER_DIM:WIN_DIM] = _dot_3pass(wfour, _dot_3pass(chan_ref[1], wf)).astype(BF16)


def _fold_weights(chan, wf_bd, wp_bd, ps, w_out, w_in, gmix_col):
    lay = lambda l: (l, 0, 0)
    return pl.pallas_call(
        _fold_kernel,
        grid=(DEPTH,),
        in_specs=[pl.BlockSpec((2, FOURIER_DIM, FOURIER_DIM), lambda l: (0, 0, 0)),
                  pl.BlockSpec((None, FOURIER_DIM, FOURIER_DIM), lay),
                  pl.BlockSpec((None, POOL_DIM, POOL_DIM), lay),
                  pl.BlockSpec((None, 1, POOL_DIM), lay),
                  pl.BlockSpec((None, D_MODEL, D_MODEL), lay),
                  pl.BlockSpec((None, D_MODEL, IN_PROJ_DIM), lay),
                  pl.BlockSpec((None, D_MODEL, 1), lay)],
        out_specs=[pl.BlockSpec((None, D_MODEL, D_MODEL), lay),
                   pl.BlockSpec((None, D_MODEL, WIN_DIM), lay)],
        out_shape=[jax.ShapeDtypeStruct((DEPTH, D_MODEL, D_MODEL), BF16),
                   jax.ShapeDtypeStruct((DEPTH, D_MODEL, WIN_DIM), BF16)],
        compiler_params=pltpu.CompilerParams(dimension_semantics=("arbitrary",),
                                             vmem_limit_bytes=VMEM_LIMIT),
        name="fold_weights",
    )(chan, wf_bd, wp_bd, ps, w_out, w_in, gmix_col)


def _window_sums(p, rows, max_half):
    sums = {1: p + pltpu.roll(p, 1, 0)}
    h = 1
    while h < max_half:
        sums[2 * h] = pltpu.roll(sums[h], h, 0) + pltpu.roll(sums[h], rows - h, 0)
        h *= 2
    return sums


def _mix_in_subtile(xe_ref, win_ref, wconv_ref, yac_ref, z_ref, out_rows):
    rows = IN_SUB + 2 * HALO
    z = _dot(xe_ref[...], win_ref[...])
    c0 = CONV_DIM
    tile = slice(0, IN_SUB)
    zp = z[:, WIN_P:WIN_B]
    z_ref[out_rows, :] = z[tile, WIN_F:WIN_DIM].astype(BF16)

    u = z[:, WIN_CH:WIN_CH + c0] * z[:, WIN_CH + c0:WIN_P]
    conv = (wconv_ref[0:1, :] * pltpu.roll(u, 1, 0) + wconv_ref[1:2, :] * u
            + wconv_ref[2:3, :] * pltpu.roll(u, rows - 1, 0))
    yac_ref[out_rows, 0:c0] = z[tile, WIN_B:WIN_F] * conv[tile]

    lane_tiles = []
    for lo in range(0, POOL_DIM, LANES):
        p = zp[:, lo:lo + LANES]
        groups = range(lo // POOL_GROUP_DIM, (lo + LANES - 1) // POOL_GROUP_DIM + 1)
        sums = _window_sums(p, rows, POOL_WINDOWS[groups[-1]] // 2)
        lane = lo + lax.broadcasted_iota(jnp.int32, (1, LANES), 1)
        wsum = sums[POOL_WINDOWS[groups[-1]] // 2][tile]
        half = jnp.full((1, LANES), POOL_WINDOWS[groups[-1]] // 2, jnp.int32)
        for grp in reversed(groups[:-1]):
            inside = lane < (grp + 1) * POOL_GROUP_DIM
            wsum = jnp.where(inside, sums[POOL_WINDOWS[grp] // 2][tile], wsum)
            half = jnp.where(inside, POOL_WINDOWS[grp] // 2, half)
        cols = slice(c0 + lo, c0 + lo + LANES)
        yac_ref[out_rows, cols] = wsum * (1.0 / (2 * half).astype(F32)) - p[tile]
        lane_tiles.append((cols, wsum, half, p))
    return lane_tiles


def _mix_in_kernel(x_ref, xprev_ref, xnext_ref, win_ref, wconv_ref, gffn_ref, w1_ref, w3_ref, w2_ref,
                   yac_ref, z_ref, w1b_ref, w3b_ref, w2b_ref, xe_ref):
    gffn = gffn_ref[...]
    w1b_ref[...] = (w1_ref[...] * gffn).astype(BF16)
    w3b_ref[...] = (w3_ref[...] * gffn).astype(BF16)
    w2b_ref[...] = w2_ref[...].astype(BF16)

    i = pl.program_id(0)
    n = pl.num_programs(0)
    prev = jnp.where(i > 0, _unit_rms(xprev_ref[...]), 0.0)
    nxt = jnp.where(i < n - 1, _unit_rms(xnext_ref[...]), 0.0)
    pieces = []
    for s in range(IN_SUBTILES):
        r0 = s * IN_SUB
        before = prev if s == 0 else _unit_rms(x_ref[r0 - HALO:r0, :])
        after = nxt if s == IN_SUBTILES - 1 else _unit_rms(x_ref[r0 + IN_SUB:r0 + IN_SUB + HALO, :])
        xe_ref[s, 0:IN_SUB, :] = _unit_rms(x_ref[r0:r0 + IN_SUB, :]).astype(BF16)
        xe_ref[s, IN_SUB:IN_SUB + 2 * HALO, :] = jnp.concatenate([after, before], axis=0).astype(BF16)
        pieces.append(_mix_in_subtile(xe_ref.at[s], win_ref, wconv_ref, yac_ref, z_ref,
                                      slice(r0, r0 + IN_SUB)))

    def clipped_window_rows(lane_tiles, local, out_rows, first_t):
        t = first_t + lax.broadcasted_iota(jnp.int32, (HALO, LANES), 0)
        for cols, wsum, half, p in lane_tiles:
            cnt = (jnp.minimum(t + half, SEQ) - jnp.maximum(t - half, 0)).astype(F32)
            yac_ref[out_rows, cols] = wsum[local] / cnt - p[local]

    @pl.when(i == 0)
    def _():
        clipped_window_rows(pieces[0], slice(0, HALO), slice(0, HALO), 0)

    @pl.when(i == n - 1)
    def _():
        clipped_window_rows(pieces[-1], slice(IN_SUB - HALO, IN_SUB), slice(TM_IN - HALO, TM_IN), SEQ - HALO)


def _mix_in(layer, x, win, wconv, gffn_col, w1, w3, w2):
    n = SEQ // TM_IN
    hb = TM_IN // HALO
    lay = lambda i: (layer, 0, 0)
    w_rows = lambda i: (layer, i, 0)
    cast_rows = lambda i: (i, 0)
    return pl.pallas_call(
        _mix_in_kernel,
        grid=(n,),
        in_specs=[
            pl.BlockSpec((TM_IN, D_MODEL), lambda i: (i, 0)),
            pl.BlockSpec((HALO, D_MODEL), lambda i: (jnp.maximum(i * hb - 1, 0), 0)),
            pl.BlockSpec((HALO, D_MODEL), lambda i: (jnp.minimum((i + 1) * hb, SEQ // HALO - 1), 0)),
            pl.BlockSpec((None, D_MODEL, WIN_DIM), lay, pipeline_mode=pl.Buffered(1)),
            pl.BlockSpec((None, 3, CONV_DIM), lay),
            pl.BlockSpec((None, D_MODEL // n, 1), w_rows),
            pl.BlockSpec((None, D_MODEL // n, D_FF), w_rows),
            pl.BlockSpec((None, D_MODEL // n, D_FF), w_rows),
            pl.BlockSpec((None, D_FF // n, D_MODEL), w_rows),
        ],
        out_specs=[pl.BlockSpec((TM_IN, YAC_DIM), lambda i: (i, 0)),
                   pl.BlockSpec((TM_IN, Z_DIM), lambda i: (i, 0)),
                   pl.BlockSpec((D_MODEL // n, D_FF), cast_rows),
                   pl.BlockSpec((D_MODEL // n, D_FF), cast_rows),
                   pl.BlockSpec((D_FF // n, D_MODEL), cast_rows)],
        out_shape=[jax.ShapeDtypeStruct((SEQ, YAC_DIM), F32),
                   jax.ShapeDtypeStruct((SEQ, Z_DIM), BF16),
                   jax.ShapeDtypeStruct((D_MODEL, D_FF), BF16),
                   jax.ShapeDtypeStruct((D_MODEL, D_FF), BF16),
                   jax.ShapeDtypeStruct((D_FF, D_MODEL), BF16)],
        scratch_shapes=[pltpu.VMEM((IN_SUBTILES, IN_SUB + 2 * HALO, D_MODEL), BF16)],
        compiler_params=pltpu.CompilerParams(dimension_semantics=("arbitrary",),
                                             vmem_limit_bytes=VMEM_LIMIT),
        name="mix_in",
    )(x, x, x, win, wconv, gffn_col, w1, w3, w2)


def _dft_a_kernel(f_ref, z_ref, t_ref, zwide_ref, twide_ref, slab_ref):
    h = FOURIER_DIM
    f = f_ref[...]
    zwide_ref[...] = z_ref[...].astype(F32)
    for j in range(DFT_COLS):
        slab_ref[...] = zwide_ref[:, j, :]
        res = _dot(f, slab_ref[...].astype(BF16))
        twide_ref[:, j, 0:h] = res[0:RADIX, 0:h] - res[RADIX:2 * RADIX, h:Z_DIM]
        twide_ref[:, j, h:Z_DIM] = res[0:RADIX, h:Z_DIM] + res[RADIX:2 * RADIX, 0:h]
    t_ref[...] = twide_ref[...].astype(BF16)


def _dft_a(f_stage1, z):
    z3 = z.reshape(RADIX, RADIX, Z_DIM)
    block = (RADIX, DFT_COLS, Z_DIM)
    return pl.pallas_call(
        _dft_a_kernel,
        grid=(RADIX // DFT_COLS,),
        in_specs=[pl.BlockSpec((2 * RADIX, RADIX), lambda t: (0, 0)),
                  pl.BlockSpec(block, lambda t: (0, t, 0))],
        out_specs=pl.BlockSpec(block, lambda t: (0, t, 0)),
        out_shape=jax.ShapeDtypeStruct((RADIX, RADIX, Z_DIM), BF16),
        scratch_shapes=[pltpu.VMEM(block, F32), pltpu.VMEM(block, F32), pltpu.VMEM((RADIX, Z_DIM), F32)],
        compiler_params=pltpu.CompilerParams(dimension_semantics=("arbitrary",)),
        name="dft_a",
    )(f_stage1, z3)


def _mix_out_kernel(x_ref, yac_ref, t_ref, cphi_ref, sphi_ref, cth_ref, sth_ref, wout_ref,
                    w1_ref, w3_ref, w2_ref, gfin_ref, out_ref, yb_ref, ybuf, *, final_norm):
    chunk = pl.program_id(0)
    part = pl.program_id(1)
    c0 = CONV_DIM
    h = FOURIER_DIM

    d_rows = pl.ds(pl.multiple_of(part * D_ROWS, D_ROWS), D_ROWS)
    cphi = cphi_ref[d_rows, :]
    sphi = sphi_ref[d_rows, :]
    for j in range(SUBLANES):
        c = chunk * SUBLANES + j
        cth = cth_ref[pl.ds(c, 1), :]
        sth = sth_ref[pl.ds(c, 1), :]
        mcos = (cphi * cth - sphi * sth).astype(BF16)
        msin = (sphi * cth + cphi * sth).astype(BF16)
        tj = t_ref[j]
        yb_ref[:, j, :] = _dot(jnp.concatenate([mcos, msin], axis=1),
                               jnp.concatenate([tj[:, 0:h], tj[:, h:Z_DIM]], axis=0))

    yac = yac_ref[...].reshape(TM_OUT, YAC_DIM)
    yb = yb_ref[...].reshape(TM_OUT, h)
    ybuf[:, 0:c0] = yac[:, 0:c0].astype(BF16)
    ybuf[:, c0:c0 + h] = yb.astype(BF16)
    ybuf[:, c0 + h:D_MODEL] = yac[:, c0:YAC_DIM].astype(BF16)

    x1 = x_ref[...].reshape(TM_OUT, D_MODEL) + _dot(ybuf[...], wout_ref[...])
    xg = x1.astype(BF16)
    r = lax.rsqrt(jnp.mean(x1 * x1, axis=-1, keepdims=True) + EPS)
    acc = None
    start = 0
    for width in FF_CHUNKS:
        cols = slice(start, start + width)
        start += width
        gate = _dot(xg, w1_ref[:, cols]) * r
        up = _dot(xg, w3_ref[:, cols])
        hid = (gate * (1.0 / (1.0 + jnp.exp(-gate))) * up).astype(BF16)
        ffn = _dot(hid, w2_ref[cols, :])
        acc = ffn if acc is None else acc + ffn
    x2 = x1 + acc * r
    if final_norm:
        x2 = _unit_rms(x2) * gfin_ref[...]
    out_ref[...] = x2.reshape(D_ROWS, SUBLANES, D_MODEL)


def _mix_out(layer, x, yac, t, tabs, wout, w1, w3, w2, gfin, final_norm):
    x3 = x.reshape(RADIX, RADIX, D_MODEL)
    yac3 = yac.reshape(RADIX, RADIX, YAC_DIM)
    const = lambda s, p: (0, 0)
    lay = lambda s, p: (layer, 0, 0)
    tile = lambda s, p: (p, s, 0)
    once = dict(pipeline_mode=pl.Buffered(1))
    out = pl.pallas_call(
        functools.partial(_mix_out_kernel, final_norm=final_norm),
        grid=(RADIX // SUBLANES, D_SPLIT),
        in_specs=[
            pl.BlockSpec((D_ROWS, SUBLANES, D_MODEL), tile),
            pl.BlockSpec((D_ROWS, SUBLANES, YAC_DIM), tile),
            pl.BlockSpec((SUBLANES, RADIX, Z_DIM), lambda s, p: (s, 0, 0)),
            pl.BlockSpec((RADIX, RADIX), const),
            pl.BlockSpec((RADIX, RADIX), const),
            pl.BlockSpec((RADIX, RADIX), const),
            pl.BlockSpec((RADIX, RADIX), const),
            pl.BlockSpec((None, D_MODEL, D_MODEL), lay, **once),
            pl.BlockSpec((D_MODEL, D_FF), const, **once),
            pl.BlockSpec((D_MODEL, D_FF), const, **once),
            pl.BlockSpec((D_FF, D_MODEL), const, **once),
            pl.BlockSpec((1, D_MODEL), const),
        ],
        out_specs=pl.BlockSpec((D_ROWS, SUBLANES, D_MODEL), tile),
        out_shape=jax.ShapeDtypeStruct((RADIX, RADIX, D_MODEL), F32),
        scratch_shapes=[pltpu.VMEM((D_ROWS, SUBLANES, FOURIER_DIM), F32),
                        pltpu.VMEM((TM_OUT, D_MODEL), BF16)],
        compiler_params=pltpu.CompilerParams(dimension_semantics=("arbitrary", "arbitrary"),
                                             vmem_limit_bytes=VMEM_LIMIT),
        name="mix_out",
    )(x3, yac3, t, tabs["cphi"], tabs["sphi"], tabs["cth"], tabs["sth"], wout, w1, w3, w2, gfin)
    return out.reshape(SEQ, D_MODEL)


def kernel(x, g_mix, w_in, w_conv, w_fourier, w_pool, pool_scale, w_out, g_ffn, w1, w3, w2, g_final):
    tabs = {k: jnp.asarray(v) for k, v in _tables().items()}
    block_diag = jax.scipy.linalg.block_diag
    wf_bd = jnp.stack([block_diag(*[w_fourier[l, h] for h in range(FOURIER_HEADS)]) for l in range(DEPTH)])
    wp_bd = jnp.stack([block_diag(*[w_pool[l, g] for g in range(len(POOL_WINDOWS))]) for l in range(DEPTH)])
    ps = pool_scale.reshape(DEPTH, 1, POOL_DIM)
    gm_col, gf_col = (v.reshape(DEPTH, D_MODEL, 1) for v in (g_mix, g_ffn))
    wout, win = _fold_weights(tabs["chan"], wf_bd, wp_bd, ps, w_out, w_in, gm_col)
    f_stage1 = tabs["f_stage1"].astype(BF16)

    xs = x.reshape(SEQ, D_MODEL)
    for l in range(DEPTH):
        yac, z, w1b, w3b, w2b = _mix_in(l, xs, win, w_conv, gf_col, w1, w3, w2)
        t = _dft_a(f_stage1, z)
        xs = _mix_out(l, xs, yac, t, tabs, wout, w1b, w3b, w2b, g_final[None, :],
                      final_norm=(l == DEPTH - 1))
    return xs.reshape(1, SEQ, D_MODEL)
```

```python
import functools

import numpy as np
import jax
import jax.numpy as jnp
from jax import lax
from jax.experimental import pallas as pl
from jax.experimental.pallas import tpu as pltpu

D_MODEL = 1024
SEQ = 16384
DEPTH = 2
HEAD_DIM = 64
CONV_DIM = 384
FOURIER_HEADS = 4
FOURIER_DIM = 256
POOL_WINDOWS = (2, 4, 8, 16)
POOL_GROUP_DIM = 96
POOL_DIM = 384
IN_PROJ_DIM = 3 * CONV_DIM + FOURIER_DIM + POOL_DIM
D_FF = 2816
EPS = 1e-6

RADIX = 128
YAC_DIM = CONV_DIM + POOL_DIM
Z_DIM = 2 * FOURIER_DIM
LANES = 128
SUBLANES = 8
DFT_COLS = 16

TM_IN = 1024
IN_SUBTILES = 1
IN_SUB = TM_IN // IN_SUBTILES
HALO = 8
D_SPLIT = 1
OUT_SUBTILES = 2
D_BLOCK = RADIX // D_SPLIT
D_ROWS = D_BLOCK // OUT_SUBTILES
TM_OUT = D_ROWS * SUBLANES
MXU_DIM = 256
FF_CHUNKS = (1536, 1280)
assert sum(FF_CHUNKS) == D_FF and all(w % MXU_DIM == 0 for w in FF_CHUNKS)
VMEM_LIMIT = 56 * 1024 * 1024

F32 = jnp.float32
BF16 = jnp.bfloat16


def _dot(a, b):
    return jnp.dot(a, b, preferred_element_type=F32)


def _unit_rms(v):
    ms = jnp.mean(v * v, axis=-1, keepdims=True)
    return v * lax.rsqrt(ms + EPS)


def _tables():
    r = np.arange(RADIX, dtype=np.float64)
    ang128 = 2.0 * np.pi * np.outer(r, r) / RADIX
    angs = 2.0 * np.pi * np.outer(r, r) / SEQ
    f_stage1 = np.concatenate([np.cos(ang128), -np.sin(ang128)], axis=0)
    e = np.arange(HEAD_DIM, dtype=np.float64)
    angc = 2.0 * np.pi * np.outer(e, e) / HEAD_DIM
    norm = 1.0 / np.sqrt(float(SEQ) * HEAD_DIM)
    eye = np.eye(FOURIER_HEADS)
    cbd = np.kron(eye, np.cos(angc)) * norm
    sbd = np.kron(eye, -np.sin(angc)) * norm
    return dict(
        f_stage1=f_stage1.astype(np.float32),
        cphi=np.cos(ang128).astype(np.float32), sphi=np.sin(ang128).astype(np.float32),
        cth=np.cos(angs).astype(np.float32), sth=np.sin(angs).astype(np.float32),
        chan=np.stack([cbd, sbd]).astype(np.float32),
    )


WIN_CH = 0
WIN_P = 2 * CONV_DIM
WIN_B = 2 * CONV_DIM + POOL_DIM
WIN_F = 3 * CONV_DIM + POOL_DIM
WIN_DIM = WIN_F + Z_DIM
assert WIN_DIM % (2 * MXU_DIM) == 0


def _dot_3pass(a, b):
    a_hi = a.astype(BF16)
    b_hi = b.astype(BF16)
    a_lo = (a - a_hi.astype(F32)).astype(BF16)
    b_lo = (b - b_hi.astype(F32)).astype(BF16)
    return _dot(a_hi, b_hi) + (_dot(a_hi, b_lo) + _dot(a_lo, b_hi))


def _fold_kernel(chan_ref, wf_ref, wp_ref, ps_ref, wout_ref, win_ref, gmix_ref, woe_ref, wine_ref):
    mixed = CONV_DIM + FOURIER_DIM
    woe_ref[0:mixed, :] = wout_ref[0:mixed, :].astype(BF16)
    woe_ref[mixed:D_MODEL, :] = _dot_3pass(wp_ref[...] * ps_ref[...], wout_ref[mixed:D_MODEL, :]).astype(BF16)
    c0 = CONV_DIM
    gain = gmix_ref[...]
    wine_ref[:, WIN_CH:WIN_P] = (win_ref[:, c0:3 * c0] * gain).astype(BF16)
    wine_ref[:, WIN_P:WIN_B] = (win_ref[:, 3 * c0 + FOURIER_DIM:IN_PROJ_DIM] * gain).astype(BF16)
    wine_ref[:, WIN_B:WIN_F] = (win_ref[:, 0:c0] * gain).astype(BF16)
    wfour = win_ref[:, 3 * c0:3 * c0 + FOURIER_DIM] * gain
    wf = wf_ref[...]
    wine_ref[:, WIN_F:WIN_F + FOURIER_DIM] = _dot_3pass(wfour, _dot_3pass(chan_ref[0], wf)).astype(BF16)
    wine_ref[:, WIN_F + FOURIER_DIM:WIN_DIM] = _dot_3pass(wfour, _dot_3pass(chan_ref[1], wf)).astype(BF16)


def _fold_weights(chan, wf_bd, wp_bd, ps, w_out, w_in, gmix_col):
    lay = lambda l: (l, 0, 0)
    return pl.pallas_call(
        _fold_kernel,
        grid=(DEPTH,),
        in_specs=[pl.BlockSpec((2, FOURIER_DIM, FOURIER_DIM), lambda l: (0, 0, 0)),
                  pl.BlockSpec((None, FOURIER_DIM, FOURIER_DIM), lay),
                  pl.BlockSpec((None, POOL_DIM, POOL_DIM), lay),
                  pl.BlockSpec((None, 1, POOL_DIM), lay),
                  pl.BlockSpec((None, D_MODEL, D_MODEL), lay),
                  pl.BlockSpec((None, D_MODEL, IN_PROJ_DIM), lay),
                  pl.BlockSpec((None, D_MODEL, 1), lay)],
        out_specs=[pl.BlockSpec((None, D_MODEL, D_MODEL), lay),
                   pl.BlockSpec((None, D_MODEL, WIN_DIM), lay)],
        out_shape=[jax.ShapeDtypeStruct((DEPTH, D_MODEL, D_MODEL), BF16),
                   jax.ShapeDtypeStruct((DEPTH, D_MODEL, WIN_DIM), BF16)],
        compiler_params=pltpu.CompilerParams(dimension_semantics=("arbitrary",),
                                             vmem_limit_bytes=VMEM_LIMIT),
        name="fold_weights",
    )(chan, wf_bd, wp_bd, ps, w_out, w_in, gmix_col)


def _window_sums(p, rows, max_half):
    sums = {1: p + pltpu.roll(p, 1, 0)}
    h = 1
    while h < max_half:
        sums[2 * h] = pltpu.roll(sums[h], h, 0) + pltpu.roll(sums[h], rows - h, 0)
        h *= 2
    return sums


def _mix_in_subtile(xe_ref, win_ref, wconv_ref, yac_ref, z_ref, out_rows):
    rows = IN_SUB + 2 * HALO
    z = _dot(xe_ref[...], win_ref[...])
    c0 = CONV_DIM
    tile = slice(0, IN_SUB)
    zp = z[:, WIN_P:WIN_B]
    z_ref[out_rows, :] = z[tile, WIN_F:WIN_DIM].astype(BF16)

    u = z[:, WIN_CH:WIN_CH + c0] * z[:, WIN_CH + c0:WIN_P]
    conv = (wconv_ref[0:1, :] * pltpu.roll(u, 1, 0) + wconv_ref[1:2, :] * u
            + wconv_ref[2:3, :] * pltpu.roll(u, rows - 1, 0))
    yac_ref[out_rows, 0:c0] = z[tile, WIN_B:WIN_F] * conv[tile]

    lane_tiles = []
    for lo in range(0, POOL_DIM, LANES):
        p = zp[:, lo:lo + LANES]
        groups = range(lo // POOL_GROUP_DIM, (lo + LANES - 1) // POOL_GROUP_DIM + 1)
        sums = _window_sums(p, rows, POOL_WINDOWS[groups[-1]] // 2)
        lane = lo + lax.broadcasted_iota(jnp.int32, (1, LANES), 1)
        wsum = sums[POOL_WINDOWS[groups[-1]] // 2][tile]
        half = jnp.full((1, LANES), POOL_WINDOWS[groups[-1]] // 2, jnp.int32)
        for grp in reversed(groups[:-1]):
            inside = lane < (grp + 1) * POOL_GROUP_DIM
            wsum = jnp.where(inside, sums[POOL_WINDOWS[grp] // 2][tile], wsum)
            half = jnp.where(inside, POOL_WINDOWS[grp] // 2, half)
        cols = slice(c0 + lo, c0 + lo + LANES)
        yac_ref[out_rows, cols] = wsum * (1.0 / (2 * half).astype(F32)) - p[tile]
        lane_tiles.append((cols, wsum, half, p))
    return lane_tiles


def _mix_in_kernel(x_ref, xprev_ref, xnext_ref, win_ref, wconv_ref, gffn_ref, w1_ref, w3_ref, w2_ref,
                   yac_ref, z_ref, w1b_ref, w3b_ref, w2b_ref, xe_ref):
    gffn = gffn_ref[...]
    w1b_ref[...] = (w1_ref[...] * gffn).astype(BF16)
    w3b_ref[...] = (w3_ref[...] * gffn).astype(BF16)
    w2b_ref[...] = w2_ref[...].astype(BF16)

    i = pl.program_id(0)
    n = pl.num_programs(0)
    prev = jnp.where(i > 0, _unit_rms(xprev_ref[...]), 0.0)
    nxt = jnp.where(i < n - 1, _unit_rms(xnext_ref[...]), 0.0)
    pieces = []
    for s in range(IN_SUBTILES):
        r0 = s * IN_SUB
        before = prev if s == 0 else _unit_rms(x_ref[r0 - HALO:r0, :])
        after = nxt if s == IN_SUBTILES - 1 else _unit_rms(x_ref[r0 + IN_SUB:r0 + IN_SUB + HALO, :])
        xe_ref[s, 0:IN_SUB, :] = _unit_rms(x_ref[r0:r0 + IN_SUB, :]).astype(BF16)
        xe_ref[s, IN_SUB:IN_SUB + 2 * HALO, :] = jnp.concatenate([after, before], axis=0).astype(BF16)
        pieces.append(_mix_in_subtile(xe_ref.at[s], win_ref, wconv_ref, yac_ref, z_ref,
                                      slice(r0, r0 + IN_SUB)))

    def clipped_window_rows(lane_tiles, local, out_rows, first_t):
        t = first_t + lax.broadcasted_iota(jnp.int32, (HALO, LANES), 0)
        for cols, wsum, half, p in lane_tiles:
            cnt = (jnp.minimum(t + half, SEQ) - jnp.maximum(t - half, 0)).astype(F32)
            yac_ref[out_rows, cols] = wsum[local] / cnt - p[local]

    @pl.when(i == 0)
    def _():
        clipped_window_rows(pieces[0], slice(0, HALO), slice(0, HALO), 0)

    @pl.when(i == n - 1)
    def _():
        clipped_window_rows(pieces[-1], slice(IN_SUB - HALO, IN_SUB), slice(TM_IN - HALO, TM_IN), SEQ - HALO)


def _mix_in(layer, x, win, wconv, gffn_col, w1, w3, w2):
    n = SEQ // TM_IN
    hb = TM_IN // HALO
    lay = lambda i: (layer, 0, 0)
    w_rows = lambda i: (layer, i, 0)
    cast_rows = lambda i: (i, 0)
    return pl.pallas_call(
        _mix_in_kernel,
        grid=(n,),
        in_specs=[
            pl.BlockSpec((TM_IN, D_MODEL), lambda i: (i, 0)),
            pl.BlockSpec((HALO, D_MODEL), lambda i: (jnp.maximum(i * hb - 1, 0), 0)),
            pl.BlockSpec((HALO, D_MODEL), lambda i: (jnp.minimum((i + 1) * hb, SEQ // HALO - 1), 0)),
            pl.BlockSpec((None, D_MODEL, WIN_DIM), lay, pipeline_mode=pl.Buffered(1)),
            pl.BlockSpec((None, 3, CONV_DIM), lay),
            pl.BlockSpec((None, D_MODEL // n, 1), w_rows),
            pl.BlockSpec((None, D_MODEL // n, D_FF), w_rows),
            pl.BlockSpec((None, D_MODEL // n, D_FF), w_rows),
            pl.BlockSpec((None, D_FF // n, D_MODEL), w_rows),
        ],
        out_specs=[pl.BlockSpec((TM_IN, YAC_DIM), lambda i: (i, 0)),
                   pl.BlockSpec((TM_IN, Z_DIM), lambda i: (i, 0)),
                   pl.BlockSpec((D_MODEL // n, D_FF), cast_rows),
                   pl.BlockSpec((D_MODEL // n, D_FF), cast_rows),
                   pl.BlockSpec((D_FF // n, D_MODEL), cast_rows)],
        out_shape=[jax.ShapeDtypeStruct((SEQ, YAC_DIM), F32),
                   jax.ShapeDtypeStruct((SEQ, Z_DIM), BF16),
                   jax.ShapeDtypeStruct((D_MODEL, D_FF), BF16),
                   jax.ShapeDtypeStruct((D_MODEL, D_FF), BF16),
                   jax.ShapeDtypeStruct((D_FF, D_MODEL), BF16)],
        scratch_shapes=[pltpu.VMEM((IN_SUBTILES, IN_SUB + 2 * HALO, D_MODEL), BF16)],
        compiler_params=pltpu.CompilerParams(dimension_semantics=("arbitrary",),
                                             vmem_limit_bytes=VMEM_LIMIT),
        name="mix_in",
    )(x, x, x, win, wconv, gffn_col, w1, w3, w2)


def _dft_a_kernel(f_ref, z_ref, t_ref, zwide_ref, twide_ref, slab_ref):
    h = FOURIER_DIM
    f = f_ref[...]
    zwide_ref[...] = z_ref[...].astype(F32)
    for j in range(DFT_COLS):
        slab_ref[...] = zwide_ref[:, j, :]
        res = _dot(f, slab_ref[...].astype(BF16))
        twide_ref[:, j, 0:h] = res[0:RADIX, 0:h] - res[RADIX:2 * RADIX, h:Z_DIM]
        twide_ref[:, j, h:Z_DIM] = res[0:RADIX, h:Z_DIM] + res[RADIX:2 * RADIX, 0:h]
    t_ref[...] = twide_ref[...].astype(BF16)


def _dft_a(f_stage1, z):
    z3 = z.reshape(RADIX, RADIX, Z_DIM)
    block = (RADIX, DFT_COLS, Z_DIM)
    return pl.pallas_call(
        _dft_a_kernel,
        grid=(RADIX // DFT_COLS,),
        in_specs=[pl.BlockSpec((2 * RADIX, RADIX), lambda t: (0, 0)),
                  pl.BlockSpec(block, lambda t: (0, t, 0))],
        out_specs=pl.BlockSpec(block, lambda t: (0, t, 0)),
        out_shape=jax.ShapeDtypeStruct((RADIX, RADIX, Z_DIM), BF16),
        scratch_shapes=[pltpu.VMEM(block, F32), pltpu.VMEM(block, F32), pltpu.VMEM((RADIX, Z_DIM), F32)],
        compiler_params=pltpu.CompilerParams(dimension_semantics=("arbitrary",)),
        name="dft_a",
    )(f_stage1, z3)


def _mix_out_kernel(x_ref, yac_ref, t_ref, cphi_ref, sphi_ref, cth_ref, sth_ref, wout_ref,
                    w1_ref, w3_ref, w2_ref, gfin_ref, out_ref, yb_ref, ybuf, *, final_norm):
    chunk = pl.program_id(0)
    part = pl.program_id(1)
    c0 = CONV_DIM
    h = FOURIER_DIM

    for sub in range(OUT_SUBTILES):
        local = slice(sub * D_ROWS, (sub + 1) * D_ROWS)
        d_rows = pl.ds(pl.multiple_of(part * D_BLOCK + sub * D_ROWS, D_ROWS), D_ROWS)
        cphi = cphi_ref[d_rows, :]
        sphi = sphi_ref[d_rows, :]
        for j in range(SUBLANES):
            c = chunk * SUBLANES + j
            cth = cth_ref[pl.ds(c, 1), :]
            sth = sth_ref[pl.ds(c, 1), :]
            mcos = (cphi * cth - sphi * sth).astype(BF16)
            msin = (sphi * cth + cphi * sth).astype(BF16)
            tj = t_ref[j]
            yb_ref[sub, :, j, :] = _dot(jnp.concatenate([mcos, msin], axis=1),
                                        jnp.concatenate([tj[:, 0:h], tj[:, h:Z_DIM]], axis=0))

        yac = yac_ref[local].reshape(TM_OUT, YAC_DIM)
        yb = yb_ref[sub].reshape(TM_OUT, h)
        ybuf[sub, :, 0:c0] = yac[:, 0:c0].astype(BF16)
        ybuf[sub, :, c0:c0 + h] = yb.astype(BF16)
        ybuf[sub, :, c0 + h:D_MODEL] = yac[:, c0:YAC_DIM].astype(BF16)

        x1 = x_ref[local].reshape(TM_OUT, D_MODEL) + _dot(ybuf[sub], wout_ref[...])
        xg = x1.astype(BF16)
        r = lax.rsqrt(jnp.mean(x1 * x1, axis=-1, keepdims=True) + EPS)
        acc = None
        start = 0
        for width in FF_CHUNKS:
            cols = slice(start, start + width)
            start += width
            gate = _dot(xg, w1_ref[:, cols]) * r
            up = _dot(xg, w3_ref[:, cols])
            hid = (gate * (1.0 / (1.0 + jnp.exp(-gate))) * up).astype(BF16)
            ffn = _dot(hid, w2_ref[cols, :])
            acc = ffn if acc is None else acc + ffn
        x2 = x1 + acc * r
        if final_norm:
            x2 = _unit_rms(x2) * gfin_ref[...]
        out_ref[local] = x2.reshape(D_ROWS, SUBLANES, D_MODEL)


def _mix_out(layer, x, yac, t, tabs, wout, w1, w3, w2, gfin, final_norm):
    x3 = x.reshape(RADIX, RADIX, D_MODEL)
    yac3 = yac.reshape(RADIX, RADIX, YAC_DIM)
    const = lambda s, p: (0, 0)
    lay = lambda s, p: (layer, 0, 0)
    tile = lambda s, p: (p, s, 0)
    once = dict(pipeline_mode=pl.Buffered(1))
    out = pl.pallas_call(
        functools.partial(_mix_out_kernel, final_norm=final_norm),
        grid=(RADIX // SUBLANES, D_SPLIT),
        in_specs=[
            pl.BlockSpec((D_BLOCK, SUBLANES, D_MODEL), tile),
            pl.BlockSpec((D_BLOCK, SUBLANES, YAC_DIM), tile),
            pl.BlockSpec((SUBLANES, RADIX, Z_DIM), lambda s, p: (s, 0, 0)),
            pl.BlockSpec((RADIX, RADIX), const),
            pl.BlockSpec((RADIX, RADIX), const),
            pl.BlockSpec((RADIX, RADIX), const),
            pl.BlockSpec((RADIX, RADIX), const),
            pl.BlockSpec((None, D_MODEL, D_MODEL), lay, **once),
            pl.BlockSpec((D_MODEL, D_FF), const, **once),
            pl.BlockSpec((D_MODEL, D_FF), const, **once),
            pl.BlockSpec((D_FF, D_MODEL), const, **once),
            pl.BlockSpec((1, D_MODEL), const),
        ],
        out_specs=pl.BlockSpec((D_BLOCK, SUBLANES, D_MODEL), tile),
        out_shape=jax.ShapeDtypeStruct((RADIX, RADIX, D_MODEL), F32),
        scratch_shapes=[pltpu.VMEM((OUT_SUBTILES, D_ROWS, SUBLANES, FOURIER_DIM), F32),
                        pltpu.VMEM((OUT_SUBTILES, TM_OUT, D_MODEL), BF16)],
        compiler_params=pltpu.CompilerParams(dimension_semantics=("arbitrary", "arbitrary"),
                                             vmem_limit_bytes=VMEM_LIMIT),
        name="mix_out",
    )(x3, yac3, t, tabs["cphi"], tabs["sphi"], tabs["cth"], tabs["sth"], wout, w1, w3, w2, gfin)
    return out.reshape(SEQ, D_MODEL)


def kernel(x, g_mix, w_in, w_conv, w_fourier, w_pool, pool_scale, w_out, g_ffn, w1, w3, w2, g_final):
    tabs = {k: jnp.asarray(v) for k, v in _tables().items()}
    block_diag = jax.scipy.linalg.block_diag
    wf_bd = jnp.stack([block_diag(*[w_fourier[l, h] for h in range(FOURIER_HEADS)]) for l in range(DEPTH)])
    wp_bd = jnp.stack([block_diag(*[w_pool[l, g] for g in range(len(POOL_WINDOWS))]) for l in range(DEPTH)])
    ps = pool_scale.reshape(DEPTH, 1, POOL_DIM)
    gm_col, gf_col = (v.reshape(DEPTH, D_MODEL, 1) for v in (g_mix, g_ffn))
    wout, win = _fold_weights(tabs["chan"], wf_bd, wp_bd, ps, w_out, w_in, gm_col)
    f_stage1 = tabs["f_stage1"].astype(BF16)

    xs = x.reshape(SEQ, D_MODEL)
    for l in range(DEPTH):
        yac, z, w1b, w3b, w2b = _mix_in(l, xs, win, w_conv, gf_col, w1, w3, w2)
        t = _dft_a(f_stage1, z)
        xs = _mix_out(l, xs, yac, t, tabs, wout, w1b, w3b, w2b, g_final[None, :],
                      final_norm=(l == DEPTH - 1))
    return xs.reshape(1, SEQ, D_MODEL)
```

```python
import functools

import numpy as np
import jax
import jax.numpy as jnp
from jax import lax
from jax.experimental import pallas as pl
from jax.experimental.pallas import tpu as pltpu

D_MODEL = 1024
SEQ = 16384
DEPTH = 2
HEAD_DIM = 64
CONV_DIM = 384
FOURIER_HEADS = 4
FOURIER_DIM = 256
POOL_WINDOWS = (2, 4, 8, 16)
POOL_GROUP_DIM = 96
POOL_DIM = 384
IN_PROJ_DIM = 3 * CONV_DIM + FOURIER_DIM + POOL_DIM
D_FF = 2816
EPS = 1e-6

RADIX = 128
YAC_DIM = CONV_DIM + POOL_DIM
Z_DIM = 2 * FOURIER_DIM
LANES = 128
SUBLANES = 8
DFT_COLS = 16

TM_IN = 1024
HALO = 8
assert all(w // 2 == 2 ** k for k, w in enumerate(POOL_WINDOWS)) and POOL_WINDOWS[-1] // 2 <= HALO
OUT_SUBTILES = 2
D_ROWS = RADIX // OUT_SUBTILES
TM_OUT = D_ROWS * SUBLANES
MXU_DIM = 256
FF_CHUNKS = (1536, 1280)
assert sum(FF_CHUNKS) == D_FF and all(w % MXU_DIM == 0 for w in FF_CHUNKS)
VMEM_LIMIT = 56 * 1024 * 1024

F32 = jnp.float32
BF16 = jnp.bfloat16


def _dot(a, b):
    return jnp.dot(a, b, preferred_element_type=F32)


def _unit_rms(v):
    ms = jnp.mean(v * v, axis=-1, keepdims=True)
    return v * lax.rsqrt(ms + EPS)


def _tables():
    r = np.arange(RADIX, dtype=np.float64)
    ang128 = 2.0 * np.pi * np.outer(r, r) / RADIX
    angs = 2.0 * np.pi * np.outer(r, r) / SEQ
    f_stage1 = np.concatenate([np.cos(ang128), -np.sin(ang128)], axis=0)
    e = np.arange(HEAD_DIM, dtype=np.float64)
    angc = 2.0 * np.pi * np.outer(e, e) / HEAD_DIM
    norm = 1.0 / np.sqrt(float(SEQ) * HEAD_DIM)
    eye = np.eye(FOURIER_HEADS)
    cbd = np.kron(eye, np.cos(angc)) * norm
    sbd = np.kron(eye, -np.sin(angc)) * norm
    return dict(
        f_stage1=f_stage1.astype(np.float32),
        cphi=np.cos(ang128).astype(np.float32), sphi=np.sin(ang128).astype(np.float32),
        cth=np.cos(angs).astype(np.float32), sth=np.sin(angs).astype(np.float32),
        chan=np.stack([cbd, sbd]).astype(np.float32),
    )


WIN_CH = 0
WIN_P = 2 * CONV_DIM
WIN_B = 2 * CONV_DIM + POOL_DIM
WIN_F = 3 * CONV_DIM + POOL_DIM
WIN_DIM = WIN_F + Z_DIM
assert WIN_DIM % (2 * MXU_DIM) == 0


def _dot_3pass(a, b):
    a_hi = a.astype(BF16)
    b_hi = b.astype(BF16)
    a_lo = (a - a_hi.astype(F32)).astype(BF16)
    b_lo = (b - b_hi.astype(F32)).astype(BF16)
    return _dot(a_hi, b_hi) + (_dot(a_hi, b_lo) + _dot(a_lo, b_hi))


def _fold_kernel(chan_ref, wf_ref, wp_ref, ps_ref, wout_ref, win_ref, gmix_ref, woe_ref, wine_ref):
    mixed = CONV_DIM + FOURIER_DIM
    woe_ref[0:mixed, :] = wout_ref[0:mixed, :].astype(BF16)
    woe_ref[mixed:D_MODEL, :] = _dot_3pass(wp_ref[...] * ps_ref[...], wout_ref[mixed:D_MODEL, :]).astype(BF16)
    c0 = CONV_DIM
    gain = gmix_ref[...]
    wine_ref[:, WIN_CH:WIN_P] = (win_ref[:, c0:3 * c0] * gain).astype(BF16)
    wine_ref[:, WIN_P:WIN_B] = (win_ref[:, 3 * c0 + FOURIER_DIM:IN_PROJ_DIM] * gain).astype(BF16)
    wine_ref[:, WIN_B:WIN_F] = (win_ref[:, 0:c0] * gain).astype(BF16)
    wfour = win_ref[:, 3 * c0:3 * c0 + FOURIER_DIM] * gain
    wf = wf_ref[...]
    wine_ref[:, WIN_F:WIN_F + FOURIER_DIM] = _dot_3pass(wfour, _dot_3pass(chan_ref[0], wf)).astype(BF16)
    wine_ref[:, WIN_F + FOURIER_DIM:WIN_DIM] = _dot_3pass(wfour, _dot_3pass(chan_ref[1], wf)).astype(BF16)


def _fold_weights(chan, wf_bd, wp_bd, ps, w_out, w_in, gmix_col):
    lay = lambda l: (l, 0, 0)
    return pl.pallas_call(
        _fold_kernel,
        grid=(DEPTH,),
        in_specs=[pl.BlockSpec((2, FOURIER_DIM, FOURIER_DIM), lambda l: (0, 0, 0)),
                  pl.BlockSpec((None, FOURIER_DIM, FOURIER_DIM), lay),
                  pl.BlockSpec((None, POOL_DIM, POOL_DIM), lay),
                  pl.BlockSpec((None, 1, POOL_DIM), lay),
                  pl.BlockSpec((None, D_MODEL, D_MODEL), lay),
                  pl.BlockSpec((None, D_MODEL, IN_PROJ_DIM), lay),
                  pl.BlockSpec((None, D_MODEL, 1), lay)],
        out_specs=[pl.BlockSpec((None, D_MODEL, D_MODEL), lay),
                   pl.BlockSpec((None, D_MODEL, WIN_DIM), lay)],
        out_shape=[jax.ShapeDtypeStruct((DEPTH, D_MODEL, D_MODEL), BF16),
                   jax.ShapeDtypeStruct((DEPTH, D_MODEL, WIN_DIM), BF16)],
        compiler_params=pltpu.CompilerParams(dimension_semantics=("arbitrary",),
                                             vmem_limit_bytes=VMEM_LIMIT),
        name="fold_weights",
    )(chan, wf_bd, wp_bd, ps, w_out, w_in, gmix_col)


def _window_sums(p, rows, max_half):
    sums = {1: p + pltpu.roll(p, 1, 0)}
    h = 1
    while h < max_half:
        sums[2 * h] = pltpu.roll(sums[h], h, 0) + pltpu.roll(sums[h], rows - h, 0)
        h *= 2
    return sums


def _mix_in_kernel(x_ref, xprev_ref, xnext_ref, win_ref, wconv_ref, gffn_ref, w1_ref, w3_ref, w2_ref,
                   yac_ref, z_ref, w1b_ref, w3b_ref, w2b_ref, xe_ref):
    gffn = gffn_ref[...]
    w1b_ref[...] = (w1_ref[...] * gffn).astype(BF16)
    w3b_ref[...] = (w3_ref[...] * gffn).astype(BF16)
    w2b_ref[...] = w2_ref[...].astype(BF16)

    i = pl.program_id(0)
    n = pl.num_programs(0)
    rows = TM_IN + 2 * HALO
    prev = jnp.where(i > 0, _unit_rms(xprev_ref[...]), 0.0)
    nxt = jnp.where(i < n - 1, _unit_rms(xnext_ref[...]), 0.0)
    xe_ref[0:TM_IN, :] = _unit_rms(x_ref[...]).astype(BF16)
    xe_ref[TM_IN:rows, :] = jnp.concatenate([nxt, prev], axis=0).astype(BF16)

    z = _dot(xe_ref[...], win_ref[...])
    c0 = CONV_DIM
    tile = slice(0, TM_IN)
    zp = z[:, WIN_P:WIN_B]
    z_ref[...] = z[tile, WIN_F:WIN_DIM].astype(BF16)

    u = z[:, WIN_CH:WIN_CH + c0] * z[:, WIN_CH + c0:WIN_P]
    conv = (wconv_ref[0:1, :] * pltpu.roll(u, 1, 0) + wconv_ref[1:2, :] * u
            + wconv_ref[2:3, :] * pltpu.roll(u, rows - 1, 0))
    yac_ref[:, 0:c0] = z[tile, WIN_B:WIN_F] * conv[tile]

    lane_tiles = []
    for lo in range(0, POOL_DIM, LANES):
        p = zp[:, lo:lo + LANES]
        groups = range(lo // POOL_GROUP_DIM, (lo + LANES - 1) // POOL_GROUP_DIM + 1)
        sums = _window_sums(p, rows, POOL_WINDOWS[groups[-1]] // 2)
        lane = lo + lax.broadcasted_iota(jnp.int32, (1, LANES), 1)
        wsum = sums[POOL_WINDOWS[groups[-1]] // 2][tile]
        half = jnp.full((1, LANES), POOL_WINDOWS[groups[-1]] // 2, jnp.int32)
        for grp in reversed(groups[:-1]):
            inside = lane < (grp + 1) * POOL_GROUP_DIM
            wsum = jnp.where(inside, sums[POOL_WINDOWS[grp] // 2][tile], wsum)
            half = jnp.where(inside, POOL_WINDOWS[grp] // 2, half)
        cols = slice(c0 + lo, c0 + lo + LANES)
        yac_ref[:, cols] = wsum * (1.0 / (2 * half).astype(F32)) - p[tile]
        lane_tiles.append((cols, wsum, half, p))

    def clipped_window_rows(local, first_t):
        t = first_t + lax.broadcasted_iota(jnp.int32, (HALO, LANES), 0)
        for cols, wsum, half, p in lane_tiles:
            cnt = (jnp.minimum(t + half, SEQ) - jnp.maximum(t - half, 0)).astype(F32)
            yac_ref[local, cols] = wsum[local] / cnt - p[local]

    @pl.when(i == 0)
    def _():
        clipped_window_rows(slice(0, HALO), 0)

    @pl.when(i == n - 1)
    def _():
        clipped_window_rows(slice(TM_IN - HALO, TM_IN), SEQ - HALO)


def _mix_in(layer, x, win, wconv, gffn_col, w1, w3, w2):
    n = SEQ // TM_IN
    hb = TM_IN // HALO
    lay = lambda i: (layer, 0, 0)
    w_rows = lambda i: (layer, i, 0)
    cast_rows = lambda i: (i, 0)
    return pl.pallas_call(
        _mix_in_kernel,
        grid=(n,),
        in_specs=[
            pl.BlockSpec((TM_IN, D_MODEL), lambda i: (i, 0)),
            pl.BlockSpec((HALO, D_MODEL), lambda i: (jnp.maximum(i * hb - 1, 0), 0)),
            pl.BlockSpec((HALO, D_MODEL), lambda i: (jnp.minimum((i + 1) * hb, SEQ // HALO - 1), 0)),
            pl.BlockSpec((None, D_MODEL, WIN_DIM), lay, pipeline_mode=pl.Buffered(1)),
            pl.BlockSpec((None, 3, CONV_DIM), lay),
            pl.BlockSpec((None, D_MODEL // n, 1), w_rows),
            pl.BlockSpec((None, D_MODEL // n, D_FF), w_rows),
            pl.BlockSpec((None, D_MODEL // n, D_FF), w_rows),
            pl.BlockSpec((None, D_FF // n, D_MODEL), w_rows),
        ],
        out_specs=[pl.BlockSpec((TM_IN, YAC_DIM), lambda i: (i, 0)),
                   pl.BlockSpec((TM_IN, Z_DIM), lambda i: (i, 0)),
                   pl.BlockSpec((D_MODEL // n, D_FF), cast_rows),
                   pl.BlockSpec((D_MODEL // n, D_FF), cast_rows),
                   pl.BlockSpec((D_FF // n, D_MODEL), cast_rows)],
        out_shape=[jax.ShapeDtypeStruct((SEQ, YAC_DIM), F32),
                   jax.ShapeDtypeStruct((SEQ, Z_DIM), BF16),
                   jax.ShapeDtypeStruct((D_MODEL, D_FF), BF16),
                   jax.ShapeDtypeStruct((D_MODEL, D_FF), BF16),
                   jax.ShapeDtypeStruct((D_FF, D_MODEL), BF16)],
        scratch_shapes=[pltpu.VMEM((TM_IN + 2 * HALO, D_MODEL), BF16)],
        compiler_params=pltpu.CompilerParams(dimension_semantics=("arbitrary",),
                                             vmem_limit_bytes=VMEM_LIMIT),
        name="mix_in",
    )(x, x, x, win, wconv, gffn_col, w1, w3, w2)


def _dft_a_kernel(f_ref, z_ref, t_ref, zwide_ref, twide_ref, slab_ref):
    h = FOURIER_DIM
    f = f_ref[...]
    zwide_ref[...] = z_ref[...].astype(F32)
    for j in range(DFT_COLS):
        slab_ref[...] = zwide_ref[:, j, :]
        res = _dot(f, slab_ref[...].astype(BF16))
        twide_ref[:, j, 0:h] = res[0:RADIX, 0:h] - res[RADIX:2 * RADIX, h:Z_DIM]
        twide_ref[:, j, h:Z_DIM] = res[0:RADIX, h:Z_DIM] + res[RADIX:2 * RADIX, 0:h]
    t_ref[...] = twide_ref[...].astype(BF16)


def _dft_a(f_stage1, z):
    z3 = z.reshape(RADIX, RADIX, Z_DIM)
    block = (RADIX, DFT_COLS, Z_DIM)
    return pl.pallas_call(
        _dft_a_kernel,
        grid=(RADIX // DFT_COLS,),
        in_specs=[pl.BlockSpec((2 * RADIX, RADIX), lambda t: (0, 0)),
                  pl.BlockSpec(block, lambda t: (0, t, 0))],
        out_specs=pl.BlockSpec(block, lambda t: (0, t, 0)),
        out_shape=jax.ShapeDtypeStruct((RADIX, RADIX, Z_DIM), BF16),
        scratch_shapes=[pltpu.VMEM(block, F32), pltpu.VMEM(block, F32), pltpu.VMEM((RADIX, Z_DIM), F32)],
        compiler_params=pltpu.CompilerParams(dimension_semantics=("arbitrary",)),
        name="dft_a",
    )(f_stage1, z3)


def _mix_out_kernel(x_ref, yac_ref, t_ref, cphi_ref, sphi_ref, cth_ref, sth_ref, wout_ref,
                    w1_ref, w3_ref, w2_ref, gfin_ref, out_ref, yb_ref, ybuf, *, final_norm):
    chunk = pl.program_id(0)
    c0 = CONV_DIM
    h = FOURIER_DIM

    for sub in range(OUT_SUBTILES):
        local = slice(sub * D_ROWS, (sub + 1) * D_ROWS)
        cphi = cphi_ref[local, :]
        sphi = sphi_ref[local, :]
        for j in range(SUBLANES):
            c = chunk * SUBLANES + j
            cth = cth_ref[pl.ds(c, 1), :]
            sth = sth_ref[pl.ds(c, 1), :]
            mcos = (cphi * cth - sphi * sth).astype(BF16)
            msin = (sphi * cth + cphi * sth).astype(BF16)
            tj = t_ref[j]
            yb_ref[sub, :, j, :] = _dot(jnp.concatenate([mcos, msin], axis=1),
                                        jnp.concatenate([tj[:, 0:h], tj[:, h:Z_DIM]], axis=0))

        yac = yac_ref[local].reshape(TM_OUT, YAC_DIM)
        yb = yb_ref[sub].reshape(TM_OUT, h)
        ybuf[sub, :, 0:c0] = yac[:, 0:c0].astype(BF16)
        ybuf[sub, :, c0:c0 + h] = yb.astype(BF16)
        ybuf[sub, :, c0 + h:D_MODEL] = yac[:, c0:YAC_DIM].astype(BF16)

        x1 = x_ref[local].reshape(TM_OUT, D_MODEL) + _dot(ybuf[sub], wout_ref[...])
        xg = x1.astype(BF16)
        r = lax.rsqrt(jnp.mean(x1 * x1, axis=-1, keepdims=True) + EPS)
        acc = None
        start = 0
        for width in FF_CHUNKS:
            cols = slice(start, start + width)
            start += width
            gate = _dot(xg, w1_ref[:, cols]) * r
            up = _dot(xg, w3_ref[:, cols])
            hid = (gate * (1.0 / (1.0 + jnp.exp(-gate))) * up).astype(BF16)
            ffn = _dot(hid, w2_ref[cols, :])
            acc = ffn if acc is None else acc + ffn
        x2 = x1 + acc * r
        if final_norm:
            x2 = _unit_rms(x2) * gfin_ref[...]
        out_ref[local] = x2.reshape(D_ROWS, SUBLANES, D_MODEL)


def _mix_out(layer, x, yac, t, tabs, wout, w1, w3, w2, gfin, final_norm):
    x3 = x.reshape(RADIX, RADIX, D_MODEL)
    yac3 = yac.reshape(RADIX, RADIX, YAC_DIM)
    const = lambda s: (0, 0)
    lay = lambda s: (layer, 0, 0)
    tile = lambda s: (0, s, 0)
    once = dict(pipeline_mode=pl.Buffered(1))
    out = pl.pallas_call(
        functools.partial(_mix_out_kernel, final_norm=final_norm),
        grid=(RADIX // SUBLANES,),
        in_specs=[
            pl.BlockSpec((RADIX, SUBLANES, D_MODEL), tile),
            pl.BlockSpec((RADIX, SUBLANES, YAC_DIM), tile),
            pl.BlockSpec((SUBLANES, RADIX, Z_DIM), lambda s: (s, 0, 0)),
            pl.BlockSpec((RADIX, RADIX), const),
            pl.BlockSpec((RADIX, RADIX), const),
            pl.BlockSpec((RADIX, RADIX), const),
            pl.BlockSpec((RADIX, RADIX), const),
            pl.BlockSpec((None, D_MODEL, D_MODEL), lay, **once),
            pl.BlockSpec((D_MODEL, D_FF), const, **once),
            pl.BlockSpec((D_MODEL, D_FF), const, **once),
            pl.BlockSpec((D_FF, D_MODEL), const, **once),
            pl.BlockSpec((1, D_MODEL), const),
        ],
        out_specs=pl.BlockSpec((RADIX, SUBLANES, D_MODEL), tile),
        out_shape=jax.ShapeDtypeStruct((RADIX, RADIX, D_MODEL), F32),
        scratch_shapes=[pltpu.VMEM((OUT_SUBTILES, D_ROWS, SUBLANES, FOURIER_DIM), F32),
                        pltpu.VMEM((OUT_SUBTILES, TM_OUT, D_MODEL), BF16)],
        compiler_params=pltpu.CompilerParams(dimension_semantics=("arbitrary",),
                                             vmem_limit_bytes=VMEM_LIMIT),
        name="mix_out",
    )(x3, yac3, t, tabs["cphi"], tabs["sphi"], tabs["cth"], tabs["sth"], wout, w1, w3, w2, gfin)
    return out.reshape(SEQ, D_MODEL)


def kernel(x, g_mix, w_in, w_conv, w_fourier, w_pool, pool_scale, w_out, g_ffn, w1, w3, w2, g_final):
    tabs = {k: jnp.asarray(v) for k, v in _tables().items()}
    block_diag = jax.scipy.linalg.block_diag
    wf_bd = jnp.stack([block_diag(*[w_fourier[l, h] for h in range(FOURIER_HEADS)]) for l in range(DEPTH)])
    wp_bd = jnp.stack([block_diag(*[w_pool[l, g] for g in range(len(POOL_WINDOWS))]) for l in range(DEPTH)])
    ps = pool_scale.reshape(DEPTH, 1, POOL_DIM)
    gm_col, gf_col = (v.reshape(DEPTH, D_MODEL, 1) for v in (g_mix, g_ffn))
    wout, win = _fold_weights(tabs["chan"], wf_bd, wp_bd, ps, w_out, w_in, gm_col)
    f_stage1 = tabs["f_stage1"].astype(BF16)

    xs = x.reshape(SEQ, D_MODEL)
    for l in range(DEPTH):
        yac, z, w1b, w3b, w2b = _mix_in(l, xs, win, w_conv, gf_col, w1, w3, w2)
        t = _dft_a(f_stage1, z)
        xs = _mix_out(l, xs, yac, t, tabs, wout, w1b, w3b, w2b, g_final[None, :],
                      final_norm=(l == DEPTH - 1))
    return xs.reshape(1, SEQ, D_MODEL)
```

```python
import functools

import numpy as np
import jax
import jax.numpy as jnp
from jax import lax
from jax.experimental import pallas as pl
from jax.experimental.pallas import tpu as pltpu

D_MODEL = 1024
SEQ = 16384
DEPTH = 2
HEAD_DIM = 64
CONV_DIM = 384
FOURIER_HEADS = 4
FOURIER_DIM = 256
POOL_WINDOWS = (2, 4, 8, 16)
POOL_GROUP_DIM = 96
POOL_DIM = 384
IN_PROJ_DIM = 3 * CONV_DIM + FOURIER_DIM + POOL_DIM
D_FF = 2816
EPS = 1e-6

RADIX = 128
YAC_DIM = CONV_DIM + POOL_DIM
Z_DIM = 2 * FOURIER_DIM
LANES = 128
SUBLANES = 8
DFT_COLS = 16

TM_IN = 1024
HALO = 8
assert all(w // 2 == 2 ** k for k, w in enumerate(POOL_WINDOWS)) and POOL_WINDOWS[-1] // 2 <= HALO
OUT_SUBTILES = 2
D_ROWS = RADIX // OUT_SUBTILES
TM_OUT = D_ROWS * SUBLANES
MXU_DIM = 256
FF_CHUNKS = (1536, 1280)
assert sum(FF_CHUNKS) == D_FF and all(w % MXU_DIM == 0 for w in FF_CHUNKS)
VMEM_LIMIT = 56 * 1024 * 1024

F32 = jnp.float32
BF16 = jnp.bfloat16


def _dot(a, b):
    return jnp.dot(a, b, preferred_element_type=F32)


def _unit_rms(v):
    ms = jnp.mean(v * v, axis=-1, keepdims=True)
    return v * lax.rsqrt(ms + EPS)


def _tables():
    r = np.arange(RADIX, dtype=np.float64)
    ang128 = 2.0 * np.pi * np.outer(r, r) / RADIX
    angs = 2.0 * np.pi * np.outer(r, r) / SEQ
    f_stage1 = np.concatenate([np.cos(ang128), -np.sin(ang128)], axis=0)
    e = np.arange(HEAD_DIM, dtype=np.float64)
    angc = 2.0 * np.pi * np.outer(e, e) / HEAD_DIM
    norm = 1.0 / np.sqrt(float(SEQ) * HEAD_DIM)
    eye = np.eye(FOURIER_HEADS)
    cbd = np.kron(eye, np.cos(angc)) * norm
    sbd = np.kron(eye, -np.sin(angc)) * norm
    return dict(
        f_stage1=f_stage1.astype(np.float32),
        cphi=np.cos(ang128).astype(np.float32), sphi=np.sin(ang128).astype(np.float32),
        cth=np.cos(angs).astype(np.float32), sth=np.sin(angs).astype(np.float32),
        chan=np.stack([cbd, sbd]).astype(np.float32),
    )


WIN_CH = 0
WIN_P = 2 * CONV_DIM
WIN_B = 2 * CONV_DIM + POOL_DIM
WIN_F = 3 * CONV_DIM + POOL_DIM
WIN_DIM = WIN_F + Z_DIM
assert WIN_DIM % (2 * MXU_DIM) == 0
FOLD_PARTS = 4
FOLD_ROWS = D_MODEL // FOLD_PARTS


def _dot_3pass(a, b):
    a_hi = a.astype(BF16)
    b_hi = b.astype(BF16)
    a_lo = (a - a_hi.astype(F32)).astype(BF16)
    b_lo = (b - b_hi.astype(F32)).astype(BF16)
    return _dot(a_hi, b_hi) + (_dot(a_hi, b_lo) + _dot(a_lo, b_hi))


def _row_as_column(row):
    diagonal = (lax.broadcasted_iota(jnp.int32, (LANES, LANES), 0)
                == lax.broadcasted_iota(jnp.int32, (LANES, LANES), 1))
    blocks = [jnp.sum(jnp.where(diagonal, row[:, k:k + LANES], 0.0), axis=1, keepdims=True)
              for k in range(0, row.shape[1], LANES)]
    return jnp.concatenate(blocks, axis=0)


def _fold_kernel(chan_ref, wf_ref, wp_ref, ps_ref, wout_ref, win_ref, gmix_ref, woe_ref, wine_ref,
                 gain_ref, ab_ref):
    layer = pl.program_id(0)
    part = pl.program_id(1)

    @pl.when(part == 0)
    def _():
        gain_ref[...] = _row_as_column(gmix_ref[pl.ds(layer, 1), :])
        wf = wf_ref[...]
        ab_ref[:, 0:FOURIER_DIM] = _dot_3pass(chan_ref[0], wf)
        ab_ref[:, FOURIER_DIM:Z_DIM] = _dot_3pass(chan_ref[1], wf)

    mixed = CONV_DIM + FOURIER_DIM
    woe_ref[0:mixed, :] = wout_ref[0:mixed, :].astype(BF16)
    woe_ref[mixed:D_MODEL, :] = _dot_3pass(wp_ref[...] * ps_ref[pl.ds(layer, 1), :],
                                           wout_ref[mixed:D_MODEL, :]).astype(BF16)
    c0 = CONV_DIM
    gain = gain_ref[pl.ds(pl.multiple_of(part * FOLD_ROWS, FOLD_ROWS), FOLD_ROWS), :]
    wine_ref[:, WIN_CH:WIN_P] = (win_ref[:, c0:3 * c0] * gain).astype(BF16)
    wine_ref[:, WIN_P:WIN_B] = (win_ref[:, 3 * c0 + FOURIER_DIM:IN_PROJ_DIM] * gain).astype(BF16)
    wine_ref[:, WIN_B:WIN_F] = (win_ref[:, 0:c0] * gain).astype(BF16)
    wfour = win_ref[:, 3 * c0:3 * c0 + FOURIER_DIM] * gain
    wine_ref[:, WIN_F:WIN_DIM] = _dot_3pass(wfour, ab_ref[...]).astype(BF16)


def _fold_weights(chan, wf_bd, wp_bd, pool_scale, w_out, w_in, g_mix):
    lay = lambda l, q: (l, 0, 0)
    whole = lambda l, q: (0, 0)
    rows = lambda l, q: (l, q, 0)
    cols = lambda l, q: (l, 0, q)
    return pl.pallas_call(
        _fold_kernel,
        grid=(DEPTH, FOLD_PARTS),
        in_specs=[pl.BlockSpec((2, FOURIER_DIM, FOURIER_DIM), lambda l, q: (0, 0, 0)),
                  pl.BlockSpec((None, FOURIER_DIM, FOURIER_DIM), lay),
                  pl.BlockSpec((None, POOL_DIM, POOL_DIM), lay),
                  pl.BlockSpec((DEPTH, POOL_DIM), whole),
                  pl.BlockSpec((None, D_MODEL, FOLD_ROWS), cols),
                  pl.BlockSpec((None, FOLD_ROWS, IN_PROJ_DIM), rows),
                  pl.BlockSpec((DEPTH, D_MODEL), whole)],
        out_specs=[pl.BlockSpec((None, D_MODEL, FOLD_ROWS), cols),
                   pl.BlockSpec((None, FOLD_ROWS, WIN_DIM), rows)],
        out_shape=[jax.ShapeDtypeStruct((DEPTH, D_MODEL, D_MODEL), BF16),
                   jax.ShapeDtypeStruct((DEPTH, D_MODEL, WIN_DIM), BF16)],
        scratch_shapes=[pltpu.VMEM((D_MODEL, 1), F32), pltpu.VMEM((FOURIER_DIM, Z_DIM), F32)],
        compiler_params=pltpu.CompilerParams(dimension_semantics=("arbitrary", "arbitrary")),
        name="fold_weights",
    )(chan, wf_bd, wp_bd, pool_scale, w_out, w_in, g_mix)


def _window_sums(p, rows, max_half):
    sums = {1: p + pltpu.roll(p, 1, 0)}
    h = 1
    while h < max_half:
        sums[2 * h] = pltpu.roll(sums[h], h, 0) + pltpu.roll(sums[h], rows - h, 0)
        h *= 2
    return sums


def _mix_in_kernel(x_ref, xprev_ref, xnext_ref, win_ref, wconv_ref, gffn_ref, w1_ref, w3_ref, w2_ref,
                   yac_ref, z_ref, w1b_ref, w3b_ref, w2b_ref, xe_ref):
    gffn = gffn_ref[...]
    w1b_ref[...] = (w1_ref[...] * gffn).astype(BF16)
    w3b_ref[...] = (w3_ref[...] * gffn).astype(BF16)
    w2b_ref[...] = w2_ref[...].astype(BF16)

    i = pl.program_id(0)
    n = pl.num_programs(0)
    rows = TM_IN + 2 * HALO
    prev = jnp.where(i > 0, _unit_rms(xprev_ref[...]), 0.0)
    nxt = jnp.where(i < n - 1, _unit_rms(xnext_ref[...]), 0.0)
    xe_ref[0:TM_IN, :] = _unit_rms(x_ref[...]).astype(BF16)
    xe_ref[TM_IN:rows, :] = jnp.concatenate([nxt, prev], axis=0).astype(BF16)

    z = _dot(xe_ref[...], win_ref[...])
    c0 = CONV_DIM
    tile = slice(0, TM_IN)
    zp = z[:, WIN_P:WIN_B]
    z_ref[...] = z[tile, WIN_F:WIN_DIM].astype(BF16)

    u = z[:, WIN_CH:WIN_CH + c0] * z[:, WIN_CH + c0:WIN_P]
    conv = (wconv_ref[0:1, :] * pltpu.roll(u, 1, 0) + wconv_ref[1:2, :] * u
            + wconv_ref[2:3, :] * pltpu.roll(u, rows - 1, 0))
    yac_ref[:, 0:c0] = z[tile, WIN_B:WIN_F] * conv[tile]

    lane_tiles = []
    for lo in range(0, POOL_DIM, LANES):
        p = zp[:, lo:lo + LANES]
        groups = range(lo // POOL_GROUP_DIM, (lo + LANES - 1) // POOL_GROUP_DIM + 1)
        sums = _window_sums(p, rows, POOL_WINDOWS[groups[-1]] // 2)
        lane = lo + lax.broadcasted_iota(jnp.int32, (1, LANES), 1)
        wsum = sums[POOL_WINDOWS[groups[-1]] // 2][tile]
        half = jnp.full((1, LANES), POOL_WINDOWS[groups[-1]] // 2, jnp.int32)
        for grp in reversed(groups[:-1]):
            inside = lane < (grp + 1) * POOL_GROUP_DIM
            wsum = jnp.where(inside, sums[POOL_WINDOWS[grp] // 2][tile], wsum)
            half = jnp.where(inside, POOL_WINDOWS[grp] // 2, half)
        cols = slice(c0 + lo, c0 + lo + LANES)
        yac_ref[:, cols] = wsum * (1.0 / (2 * half).astype(F32)) - p[tile]
        lane_tiles.append((cols, wsum, half, p))

    def clipped_window_rows(local, first_t):
        t = first_t + lax.broadcasted_iota(jnp.int32, (HALO, LANES), 0)
        for cols, wsum, half, p in lane_tiles:
            cnt = (jnp.minimum(t + half, SEQ) - jnp.maximum(t - half, 0)).astype(F32)
            yac_ref[local, cols] = wsum[local] / cnt - p[local]

    @pl.when(i == 0)
    def _():
        clipped_window_rows(slice(0, HALO), 0)

    @pl.when(i == n - 1)
    def _():
        clipped_window_rows(slice(TM_IN - HALO, TM_IN), SEQ - HALO)


def _mix_in(layer, x, win, wconv, gffn_col, w1, w3, w2):
    n = SEQ // TM_IN
    hb = TM_IN // HALO
    lay = lambda i: (layer, 0, 0)
    w_rows = lambda i: (layer, i, 0)
    cast_rows = lambda i: (i, 0)
    return pl.pallas_call(
        _mix_in_kernel,
        grid=(n,),
        in_specs=[
            pl.BlockSpec((TM_IN, D_MODEL), lambda i: (i, 0)),
            pl.BlockSpec((HALO, D_MODEL), lambda i: (jnp.maximum(i * hb - 1, 0), 0)),
            pl.BlockSpec((HALO, D_MODEL), lambda i: (jnp.minimum((i + 1) * hb, SEQ // HALO - 1), 0)),
            pl.BlockSpec((None, D_MODEL, WIN_DIM), lay, pipeline_mode=pl.Buffered(1)),
            pl.BlockSpec((None, 3, CONV_DIM), lay),
            pl.BlockSpec((None, D_MODEL // n, 1), w_rows),
            pl.BlockSpec((None, D_MODEL // n, D_FF), w_rows),
            pl.BlockSpec((None, D_MODEL // n, D_FF), w_rows),
            pl.BlockSpec((None, D_FF // n, D_MODEL), w_rows),
        ],
        out_specs=[pl.BlockSpec((TM_IN, YAC_DIM), lambda i: (i, 0)),
                   pl.BlockSpec((TM_IN, Z_DIM), lambda i: (i, 0)),
                   pl.BlockSpec((D_MODEL // n, D_FF), cast_rows),
                   pl.BlockSpec((D_MODEL // n, D_FF), cast_rows),
                   pl.BlockSpec((D_FF // n, D_MODEL), cast_rows)],
        out_shape=[jax.ShapeDtypeStruct((SEQ, YAC_DIM), F32),
                   jax.ShapeDtypeStruct((SEQ, Z_DIM), BF16),
                   jax.ShapeDtypeStruct((D_MODEL, D_FF), BF16),
                   jax.ShapeDtypeStruct((D_MODEL, D_FF), BF16),
                   jax.ShapeDtypeStruct((D_FF, D_MODEL), BF16)],
        scratch_shapes=[pltpu.VMEM((TM_IN + 2 * HALO, D_MODEL), BF16)],
        compiler_params=pltpu.CompilerParams(dimension_semantics=("arbitrary",),
                                             vmem_limit_bytes=VMEM_LIMIT),
        name="mix_in",
    )(x, x, x, win, wconv, gffn_col, w1, w3, w2)


def _dft_a_kernel(f_ref, z_ref, t_ref, zwide_ref, twide_ref, slab_ref):
    h = FOURIER_DIM
    f = f_ref[...]
    zwide_ref[...] = z_ref[...].astype(F32)
    for j in range(DFT_COLS):
        slab_ref[...] = zwide_ref[:, j, :]
        res = _dot(f, slab_ref[...].astype(BF16))
        twide_ref[:, j, 0:h] = res[0:RADIX, 0:h] - res[RADIX:2 * RADIX, h:Z_DIM]
        twide_ref[:, j, h:Z_DIM] = res[0:RADIX, h:Z_DIM] + res[RADIX:2 * RADIX, 0:h]
    t_ref[...] = twide_ref[...].astype(BF16)


def _dft_a(f_stage1, z):
    z3 = z.reshape(RADIX, RADIX, Z_DIM)
    block = (RADIX, DFT_COLS, Z_DIM)
    return pl.pallas_call(
        _dft_a_kernel,
        grid=(RADIX // DFT_COLS,),
        in_specs=[pl.BlockSpec((2 * RADIX, RADIX), lambda t: (0, 0)),
                  pl.BlockSpec(block, lambda t: (0, t, 0))],
        out_specs=pl.BlockSpec(block, lambda t: (0, t, 0)),
        out_shape=jax.ShapeDtypeStruct((RADIX, RADIX, Z_DIM), BF16),
        scratch_shapes=[pltpu.VMEM(block, F32), pltpu.VMEM(block, F32), pltpu.VMEM((RADIX, Z_DIM), F32)],
        compiler_params=pltpu.CompilerParams(dimension_semantics=("arbitrary",)),
        name="dft_a",
    )(f_stage1, z3)


def _mix_out_kernel(x_ref, yac_ref, t_ref, cphi_ref, sphi_ref, cth_ref, sth_ref, wout_ref,
                    w1_ref, w3_ref, w2_ref, gfin_ref, out_ref, yb_ref, ybuf, *, final_norm):
    chunk = pl.program_id(0)
    c0 = CONV_DIM
    h = FOURIER_DIM

    for sub in range(OUT_SUBTILES):
        local = slice(sub * D_ROWS, (sub + 1) * D_ROWS)
        cphi = cphi_ref[local, :]
        sphi = sphi_ref[local, :]
        for j in range(SUBLANES):
            c = chunk * SUBLANES + j
            cth = cth_ref[pl.ds(c, 1), :]
            sth = sth_ref[pl.ds(c, 1), :]
            mcos = (cphi * cth - sphi * sth).astype(BF16)
            msin = (sphi * cth + cphi * sth).astype(BF16)
            tj = t_ref[j]
            yb_ref[sub, :, j, :] = _dot(jnp.concatenate([mcos, msin], axis=1),
                                        jnp.concatenate([tj[:, 0:h], tj[:, h:Z_DIM]], axis=0))

        yac = yac_ref[local].reshape(TM_OUT, YAC_DIM)
        yb = yb_ref[sub].reshape(TM_OUT, h)
        ybuf[sub, :, 0:c0] = yac[:, 0:c0].astype(BF16)
        ybuf[sub, :, c0:c0 + h] = yb.astype(BF16)
        ybuf[sub, :, c0 + h:D_MODEL] = yac[:, c0:YAC_DIM].astype(BF16)

        x1 = x_ref[local].reshape(TM_OUT, D_MODEL) + _dot(ybuf[sub], wout_ref[...])
        xg = x1.astype(BF16)
        r = lax.rsqrt(jnp.mean(x1 * x1, axis=-1, keepdims=True) + EPS)
        acc = None
        start = 0
        for width in FF_CHUNKS:
            cols = slice(start, start + width)
            start += width
            gate = _dot(xg, w1_ref[:, cols]) * r
            up = _dot(xg, w3_ref[:, cols])
            hid = (gate * (1.0 / (1.0 + jnp.exp(-gate))) * up).astype(BF16)
            ffn = _dot(hid, w2_ref[cols, :])
            acc = ffn if acc is None else acc + ffn
        x2 = x1 + acc * r
        if final_norm:
            x2 = _unit_rms(x2) * gfin_ref[...]
        out_ref[local] = x2.reshape(D_ROWS, SUBLANES, D_MODEL)


def _mix_out(layer, x, yac, t, tabs, wout, w1, w3, w2, gfin, final_norm):
    x3 = x.reshape(RADIX, RADIX, D_MODEL)
    yac3 = yac.reshape(RADIX, RADIX, YAC_DIM)
    const = lambda s: (0, 0)
    lay = lambda s: (layer, 0, 0)
    tile = lambda s: (0, s, 0)
    once = dict(pipeline_mode=pl.Buffered(1))
    out = pl.pallas_call(
        functools.partial(_mix_out_kernel, final_norm=final_norm),
        grid=(RADIX // SUBLANES,),
        in_specs=[
            pl.BlockSpec((RADIX, SUBLANES, D_MODEL), tile),
            pl.BlockSpec((RADIX, SUBLANES, YAC_DIM), tile),
            pl.BlockSpec((SUBLANES, RADIX, Z_DIM), lambda s: (s, 0, 0)),
            pl.BlockSpec((RADIX, RADIX), const),
            pl.BlockSpec((RADIX, RADIX), const),
            pl.BlockSpec((RADIX, RADIX), const),
            pl.BlockSpec((RADIX, RADIX), const),
            pl.BlockSpec((None, D_MODEL, D_MODEL), lay, **once),
            pl.BlockSpec((D_MODEL, D_FF), const, **once),
            pl.BlockSpec((D_MODEL, D_FF), const, **once),
            pl.BlockSpec((D_FF, D_MODEL), const, **once),
            pl.BlockSpec((1, D_MODEL), const),
        ],
        out_specs=pl.BlockSpec((RADIX, SUBLANES, D_MODEL), tile),
        out_shape=jax.ShapeDtypeStruct((RADIX, RADIX, D_MODEL), F32),
        scratch_shapes=[pltpu.VMEM((OUT_SUBTILES, D_ROWS, SUBLANES, FOURIER_DIM), F32),
                        pltpu.VMEM((OUT_SUBTILES, TM_OUT, D_MODEL), BF16)],
        compiler_params=pltpu.CompilerParams(dimension_semantics=("arbitrary",),
                                             vmem_limit_bytes=VMEM_LIMIT),
        name="mix_out",
    )(x3, yac3, t, tabs["cphi"], tabs["sphi"], tabs["cth"], tabs["sth"], wout, w1, w3, w2, gfin)
    return out.reshape(SEQ, D_MODEL)


def kernel(x, g_mix, w_in, w_conv, w_fourier, w_pool, pool_scale, w_out, g_ffn, w1, w3, w2, g_final):
    tabs = {k: jnp.asarray(v) for k, v in _tables().items()}
    block_diag = jax.scipy.linalg.block_diag
    wf_bd = jnp.stack([block_diag(*[w_fourier[l, h] for h in range(FOURIER_HEADS)]) for l in range(DEPTH)])
    wp_bd = jnp.stack([block_diag(*[w_pool[l, g] for g in range(len(POOL_WINDOWS))]) for l in range(DEPTH)])
    gf_col = g_ffn.reshape(DEPTH, D_MODEL, 1)
    wout, win = _fold_weights(tabs["chan"], wf_bd, wp_bd, pool_scale, w_out, w_in, g_mix)
    f_stage1 = tabs["f_stage1"].astype(BF16)

    xs = x.reshape(SEQ, D_MODEL)
    for l in range(DEPTH):
        yac, z, w1b, w3b, w2b = _mix_in(l, xs, win, w_conv, gf_col, w1, w3, w2)
        t = _dft_a(f_stage1, z)
        xs = _mix_out(l, xs, yac, t, tabs, wout, w1b, w3b, w2b, g_final[None, :],
                      final_norm=(l == DEPTH - 1))
    return xs.reshape(1, SEQ, D_MODEL)
```

```python
import functools

import numpy as np
import jax
import jax.numpy as jnp
from jax import lax
from jax.experimental import pallas as pl
from jax.experimental.pallas import tpu as pltpu

D_MODEL = 1024
SEQ = 16384
DEPTH = 2
HEAD_DIM = 64
CONV_DIM = 384
FOURIER_HEADS = 4
FOURIER_DIM = 256
POOL_WINDOWS = (2, 4, 8, 16)
POOL_GROUP_DIM = 96
POOL_DIM = 384
IN_PROJ_DIM = 3 * CONV_DIM + FOURIER_DIM + POOL_DIM
D_FF = 2816
EPS = 1e-6

RADIX = 128
YAC_DIM = CONV_DIM + POOL_DIM
Z_DIM = 2 * FOURIER_DIM
LANES = 128
SUBLANES = 8
DFT_COLS = 16

TM_IN = 1024
HALO = 8
assert all(w // 2 == 2 ** k for k, w in enumerate(POOL_WINDOWS)) and POOL_WINDOWS[-1] // 2 <= HALO
OUT_SUBTILES = 2
D_ROWS = RADIX // OUT_SUBTILES
TM_OUT = D_ROWS * SUBLANES
MXU_DIM = 256
FF_CHUNKS = (1536, 1280)
assert sum(FF_CHUNKS) == D_FF and all(w % MXU_DIM == 0 for w in FF_CHUNKS)
VMEM_LIMIT = 56 * 1024 * 1024

F32 = jnp.float32
BF16 = jnp.bfloat16


def _dot(a, b):
    return jnp.dot(a, b, preferred_element_type=F32)


def _unit_rms(v):
    ms = jnp.mean(v * v, axis=-1, keepdims=True)
    return v * lax.rsqrt(ms + EPS)


def _tables():
    r = np.arange(RADIX, dtype=np.float64)
    ang128 = 2.0 * np.pi * np.outer(r, r) / RADIX
    angs = 2.0 * np.pi * np.outer(r, r) / SEQ
    f_stage1 = np.concatenate([np.cos(ang128), -np.sin(ang128)], axis=0)
    e = np.arange(HEAD_DIM, dtype=np.float64)
    angc = 2.0 * np.pi * np.outer(e, e) / HEAD_DIM
    norm = 1.0 / np.sqrt(float(SEQ) * HEAD_DIM)
    eye = np.eye(FOURIER_HEADS)
    cbd = np.kron(eye, np.cos(angc)) * norm
    sbd = np.kron(eye, -np.sin(angc)) * norm
    return dict(
        f_stage1=f_stage1.astype(np.float32),
        cphi=np.cos(ang128).astype(np.float32), sphi=np.sin(ang128).astype(np.float32),
        cth=np.cos(angs).astype(np.float32), sth=np.sin(angs).astype(np.float32),
        chan=np.stack([cbd, sbd]).astype(np.float32),
    )


WIN_CH = 0
WIN_P = 2 * CONV_DIM
WIN_B = 2 * CONV_DIM + POOL_DIM
WIN_F = 3 * CONV_DIM + POOL_DIM
WIN_DIM = WIN_F + Z_DIM
assert WIN_DIM % (2 * MXU_DIM) == 0
FOLD_PARTS = 4
FOLD_ROWS = D_MODEL // FOLD_PARTS


def _dot_3pass(a, b):
    a_hi = a.astype(BF16)
    b_hi = b.astype(BF16)
    a_lo = (a - a_hi.astype(F32)).astype(BF16)
    b_lo = (b - b_hi.astype(F32)).astype(BF16)
    return _dot(a_hi, b_hi) + (_dot(a_hi, b_lo) + _dot(a_lo, b_hi))


def _row_as_column(row):
    diagonal = (lax.broadcasted_iota(jnp.int32, (LANES, LANES), 0)
                == lax.broadcasted_iota(jnp.int32, (LANES, LANES), 1))
    blocks = [jnp.sum(jnp.where(diagonal, row[:, k:k + LANES], 0.0), axis=1, keepdims=True)
              for k in range(0, row.shape[1], LANES)]
    return jnp.concatenate(blocks, axis=0)


def _fold_kernel(chan_ref, wf_ref, wp_ref, ps_ref, wout_ref, win_ref, gmix_ref, woe_ref, wine_ref,
                 gain_ref, ab_ref):
    layer = pl.program_id(0)
    part = pl.program_id(1)

    @pl.when(part == 0)
    def _():
        gain_ref[...] = _row_as_column(gmix_ref[pl.ds(layer, 1), :])
        wf = wf_ref[...]
        ab_ref[:, 0:FOURIER_DIM] = _dot_3pass(chan_ref[0], wf)
        ab_ref[:, FOURIER_DIM:Z_DIM] = _dot_3pass(chan_ref[1], wf)

    mixed = CONV_DIM + FOURIER_DIM
    woe_ref[0:mixed, :] = wout_ref[0:mixed, :].astype(BF16)
    woe_ref[mixed:D_MODEL, :] = _dot_3pass(wp_ref[...] * ps_ref[pl.ds(layer, 1), :],
                                           wout_ref[mixed:D_MODEL, :]).astype(BF16)
    c0 = CONV_DIM
    gain = gain_ref[pl.ds(pl.multiple_of(part * FOLD_ROWS, FOLD_ROWS), FOLD_ROWS), :]
    wine_ref[:, WIN_CH:WIN_P] = (win_ref[:, c0:3 * c0] * gain).astype(BF16)
    wine_ref[:, WIN_P:WIN_B] = (win_ref[:, 3 * c0 + FOURIER_DIM:IN_PROJ_DIM] * gain).astype(BF16)
    wine_ref[:, WIN_B:WIN_F] = (win_ref[:, 0:c0] * gain).astype(BF16)
    wfour = win_ref[:, 3 * c0:3 * c0 + FOURIER_DIM] * gain
    wine_ref[:, WIN_F:WIN_DIM] = _dot_3pass(wfour, ab_ref[...]).astype(BF16)


def _fold_weights(chan, wf_bd, wp_bd, pool_scale, w_out, w_in, g_mix):
    lay = lambda l, q: (l, 0, 0)
    whole = lambda l, q: (0, 0)
    rows = lambda l, q: (l, q, 0)
    cols = lambda l, q: (l, 0, q)
    return pl.pallas_call(
        _fold_kernel,
        grid=(DEPTH, FOLD_PARTS),
        in_specs=[pl.BlockSpec((2, FOURIER_DIM, FOURIER_DIM), lambda l, q: (0, 0, 0)),
                  pl.BlockSpec((None, FOURIER_DIM, FOURIER_DIM), lay),
                  pl.BlockSpec((None, POOL_DIM, POOL_DIM), lay),
                  pl.BlockSpec((DEPTH, POOL_DIM), whole),
                  pl.BlockSpec((None, D_MODEL, FOLD_ROWS), cols),
                  pl.BlockSpec((None, FOLD_ROWS, IN_PROJ_DIM), rows),
                  pl.BlockSpec((DEPTH, D_MODEL), whole)],
        out_specs=[pl.BlockSpec((None, D_MODEL, FOLD_ROWS), cols),
                   pl.BlockSpec((None, FOLD_ROWS, WIN_DIM), rows)],
        out_shape=[jax.ShapeDtypeStruct((DEPTH, D_MODEL, D_MODEL), BF16),
                   jax.ShapeDtypeStruct((DEPTH, D_MODEL, WIN_DIM), BF16)],
        scratch_shapes=[pltpu.VMEM((D_MODEL, 1), F32), pltpu.VMEM((FOURIER_DIM, Z_DIM), F32)],
        compiler_params=pltpu.CompilerParams(dimension_semantics=("arbitrary", "arbitrary")),
        name="fold_weights",
    )(chan, wf_bd, wp_bd, pool_scale, w_out, w_in, g_mix)


def _window_sums(p, rows, max_half):
    sums = {1: p + pltpu.roll(p, 1, 0)}
    h = 1
    while h < max_half:
        sums[2 * h] = pltpu.roll(sums[h], h, 0) + pltpu.roll(sums[h], rows - h, 0)
        h *= 2
    return sums


def _mix_in_kernel(x_ref, xprev_ref, xnext_ref, win_ref, wconv_ref, gffn_ref, w1_ref, w3_ref, w2_ref,
                   yac_ref, z_ref, w1b_ref, w3b_ref, w2b_ref, xe_ref):
    gffn = gffn_ref[...]
    w1b_ref[...] = (w1_ref[...] * gffn).astype(BF16)
    w3b_ref[...] = (w3_ref[...] * gffn).astype(BF16)
    w2b_ref[...] = w2_ref[...].astype(BF16)

    i = pl.program_id(0)
    n = pl.num_programs(0)
    rows = TM_IN + 2 * HALO
    prev = jnp.where(i > 0, _unit_rms(xprev_ref[...]), 0.0)
    nxt = jnp.where(i < n - 1, _unit_rms(xnext_ref[...]), 0.0)
    xe_ref[0:TM_IN, :] = _unit_rms(x_ref[...]).astype(BF16)
    xe_ref[TM_IN:rows, :] = jnp.concatenate([nxt, prev], axis=0).astype(BF16)

    z = _dot(xe_ref[...], win_ref[...])
    c0 = CONV_DIM
    tile = slice(0, TM_IN)
    zp = z[:, WIN_P:WIN_B]
    z_ref[...] = z[tile, WIN_F:WIN_DIM].astype(BF16)

    u = z[:, WIN_CH:WIN_CH + c0] * z[:, WIN_CH + c0:WIN_P]
    conv = (wconv_ref[0:1, :] * pltpu.roll(u, 1, 0) + wconv_ref[1:2, :] * u
            + wconv_ref[2:3, :] * pltpu.roll(u, rows - 1, 0))
    yac_ref[:, 0:c0] = z[tile, WIN_B:WIN_F] * conv[tile]

    lane_tiles = []
    for lo in range(0, POOL_DIM, LANES):
        p = zp[:, lo:lo + LANES]
        groups = range(lo // POOL_GROUP_DIM, (lo + LANES - 1) // POOL_GROUP_DIM + 1)
        sums = _window_sums(p, rows, POOL_WINDOWS[groups[-1]] // 2)
        lane = lo + lax.broadcasted_iota(jnp.int32, (1, LANES), 1)
        wsum = sums[POOL_WINDOWS[groups[-1]] // 2][tile]
        half = jnp.full((1, LANES), POOL_WINDOWS[groups[-1]] // 2, jnp.int32)
        for grp in reversed(groups[:-1]):
            inside = lane < (grp + 1) * POOL_GROUP_DIM
            wsum = jnp.where(inside, sums[POOL_WINDOWS[grp] // 2][tile], wsum)
            half = jnp.where(inside, POOL_WINDOWS[grp] // 2, half)
        cols = slice(c0 + lo, c0 + lo + LANES)
        yac_ref[:, cols] = wsum * (1.0 / (2 * half).astype(F32)) - p[tile]
        lane_tiles.append((cols, wsum, half, p))

    def clipped_window_rows(local, first_t):
        t = first_t + lax.broadcasted_iota(jnp.int32, (HALO, LANES), 0)
        for cols, wsum, half, p in lane_tiles:
            cnt = (jnp.minimum(t + half, SEQ) - jnp.maximum(t - half, 0)).astype(F32)
            yac_ref[local, cols] = wsum[local] / cnt - p[local]

    @pl.when(i == 0)
    def _():
        clipped_window_rows(slice(0, HALO), 0)

    @pl.when(i == n - 1)
    def _():
        clipped_window_rows(slice(TM_IN - HALO, TM_IN), SEQ - HALO)


def _mix_in(layer, x, win, wconv, gffn_col, w1, w3, w2):
    n = SEQ // TM_IN
    hb = TM_IN // HALO
    lay = lambda i: (layer, 0, 0)
    w_rows = lambda i: (layer, i, 0)
    cast_rows = lambda i: (i, 0)
    return pl.pallas_call(
        _mix_in_kernel,
        grid=(n,),
        in_specs=[
            pl.BlockSpec((TM_IN, D_MODEL), lambda i: (i, 0)),
            pl.BlockSpec((HALO, D_MODEL), lambda i: (jnp.maximum(i * hb - 1, 0), 0)),
            pl.BlockSpec((HALO, D_MODEL), lambda i: (jnp.minimum((i + 1) * hb, SEQ // HALO - 1), 0)),
            pl.BlockSpec((None, D_MODEL, WIN_DIM), lay, pipeline_mode=pl.Buffered(1)),
            pl.BlockSpec((None, 3, CONV_DIM), lay),
            pl.BlockSpec((None, D_MODEL // n, 1), w_rows),
            pl.BlockSpec((None, D_MODEL // n, D_FF), w_rows),
            pl.BlockSpec((None, D_MODEL // n, D_FF), w_rows),
            pl.BlockSpec((None, D_FF // n, D_MODEL), w_rows),
        ],
        out_specs=[pl.BlockSpec((TM_IN, YAC_DIM), lambda i: (i, 0)),
                   pl.BlockSpec((TM_IN, Z_DIM), lambda i: (i, 0)),
                   pl.BlockSpec((D_MODEL // n, D_FF), cast_rows),
                   pl.BlockSpec((D_MODEL // n, D_FF), cast_rows),
                   pl.BlockSpec((D_FF // n, D_MODEL), cast_rows)],
        out_shape=[jax.ShapeDtypeStruct((SEQ, YAC_DIM), F32),
                   jax.ShapeDtypeStruct((SEQ, Z_DIM), BF16),
                   jax.ShapeDtypeStruct((D_MODEL, D_FF), BF16),
                   jax.ShapeDtypeStruct((D_MODEL, D_FF), BF16),
                   jax.ShapeDtypeStruct((D_FF, D_MODEL), BF16)],
        scratch_shapes=[pltpu.VMEM((TM_IN + 2 * HALO, D_MODEL), BF16)],
        compiler_params=pltpu.CompilerParams(dimension_semantics=("arbitrary",),
                                             vmem_limit_bytes=VMEM_LIMIT),
        name="mix_in",
    )(x, x, x, win, wconv, gffn_col, w1, w3, w2)


def _dft_a_kernel(f_ref, z_ref, t_ref, zslab_ref, tslab_ref):
    h = FOURIER_DIM
    f = f_ref[...]
    zslab_ref[...] = jnp.swapaxes(z_ref[...].astype(F32), 0, 1)
    for j in range(DFT_COLS):
        res = _dot(f, zslab_ref[j].astype(BF16))
        tslab_ref[j, :, 0:h] = res[0:RADIX, 0:h] - res[RADIX:2 * RADIX, h:Z_DIM]
        tslab_ref[j, :, h:Z_DIM] = res[0:RADIX, h:Z_DIM] + res[RADIX:2 * RADIX, 0:h]
    t_ref[...] = jnp.swapaxes(tslab_ref[...], 0, 1).astype(BF16)


def _dft_a(f_stage1, z):
    z3 = z.reshape(RADIX, RADIX, Z_DIM)
    block = (RADIX, DFT_COLS, Z_DIM)
    return pl.pallas_call(
        _dft_a_kernel,
        grid=(RADIX // DFT_COLS,),
        in_specs=[pl.BlockSpec((2 * RADIX, RADIX), lambda t: (0, 0)),
                  pl.BlockSpec(block, lambda t: (0, t, 0))],
        out_specs=pl.BlockSpec(block, lambda t: (0, t, 0)),
        out_shape=jax.ShapeDtypeStruct((RADIX, RADIX, Z_DIM), BF16),
        scratch_shapes=[pltpu.VMEM((DFT_COLS, RADIX, Z_DIM), F32), pltpu.VMEM((DFT_COLS, RADIX, Z_DIM), F32)],
        compiler_params=pltpu.CompilerParams(dimension_semantics=("arbitrary",)),
        name="dft_a",
    )(f_stage1, z3)


def _mix_out_kernel(x_ref, yac_ref, t_ref, cphi_ref, sphi_ref, cth_ref, sth_ref, wout_ref,
                    w1_ref, w3_ref, w2_ref, gfin_ref, out_ref, yb_ref, ybuf, *, final_norm):
    chunk = pl.program_id(0)
    c0 = CONV_DIM
    h = FOURIER_DIM

    for sub in range(OUT_SUBTILES):
        local = slice(sub * D_ROWS, (sub + 1) * D_ROWS)
        cphi = cphi_ref[local, :]
        sphi = sphi_ref[local, :]
        for j in range(SUBLANES):
            c = chunk * SUBLANES + j
            cth = cth_ref[pl.ds(c, 1), :]
            sth = sth_ref[pl.ds(c, 1), :]
            mcos = (cphi * cth - sphi * sth).astype(BF16)
            msin = (sphi * cth + cphi * sth).astype(BF16)
            tj = t_ref[j]
            yb_ref[sub, :, j, :] = _dot(jnp.concatenate([mcos, msin], axis=1),
                                        jnp.concatenate([tj[:, 0:h], tj[:, h:Z_DIM]], axis=0))

        yac = yac_ref[local].reshape(TM_OUT, YAC_DIM)
        yb = yb_ref[sub].reshape(TM_OUT, h)
        ybuf[sub, :, 0:c0] = yac[:, 0:c0].astype(BF16)
        ybuf[sub, :, c0:c0 + h] = yb.astype(BF16)
        ybuf[sub, :, c0 + h:D_MODEL] = yac[:, c0:YAC_DIM].astype(BF16)

        x1 = x_ref[local].reshape(TM_OUT, D_MODEL) + _dot(ybuf[sub], wout_ref[...])
        xg = x1.astype(BF16)
        r = lax.rsqrt(jnp.mean(x1 * x1, axis=-1, keepdims=True) + EPS)
        acc = None
        start = 0
        for width in FF_CHUNKS:
            cols = slice(start, start + width)
            start += width
            gate = _dot(xg, w1_ref[:, cols]) * r
            up = _dot(xg, w3_ref[:, cols])
            hid = (gate * (1.0 / (1.0 + jnp.exp(-gate))) * up).astype(BF16)
            ffn = _dot(hid, w2_ref[cols, :])
            acc = ffn if acc is None else acc + ffn
        x2 = x1 + acc * r
        if final_norm:
            x2 = _unit_rms(x2) * gfin_ref[...]
        out_ref[local] = x2.reshape(D_ROWS, SUBLANES, D_MODEL)


def _mix_out(layer, x, yac, t, tabs, wout, w1, w3, w2, gfin, final_norm):
    x3 = x.reshape(RADIX, RADIX, D_MODEL)
    yac3 = yac.reshape(RADIX, RADIX, YAC_DIM)
    const = lambda s: (0, 0)
    lay = lambda s: (layer, 0, 0)
    tile = lambda s: (0, s, 0)
    once = dict(pipeline_mode=pl.Buffered(1))
    out = pl.pallas_call(
        functools.partial(_mix_out_kernel, final_norm=final_norm),
        grid=(RADIX // SUBLANES,),
        in_specs=[
            pl.BlockSpec((RADIX, SUBLANES, D_MODEL), tile),
            pl.BlockSpec((RADIX, SUBLANES, YAC_DIM), tile),
            pl.BlockSpec((SUBLANES, RADIX, Z_DIM), lambda s: (s, 0, 0)),
            pl.BlockSpec((RADIX, RADIX), const),
            pl.BlockSpec((RADIX, RADIX), const),
            pl.BlockSpec((RADIX, RADIX), const),
            pl.BlockSpec((RADIX, RADIX), const),
            pl.BlockSpec((None, D_MODEL, D_MODEL), lay, **once),
            pl.BlockSpec((D_MODEL, D_FF), const, **once),
            pl.BlockSpec((D_MODEL, D_FF), const, **once),
            pl.BlockSpec((D_FF, D_MODEL), const, **once),
            pl.BlockSpec((1, D_MODEL), const),
        ],
        out_specs=pl.BlockSpec((RADIX, SUBLANES, D_MODEL), tile),
        out_shape=jax.ShapeDtypeStruct((RADIX, RADIX, D_MODEL), F32),
        scratch_shapes=[pltpu.VMEM((OUT_SUBTILES, D_ROWS, SUBLANES, FOURIER_DIM), F32),
                        pltpu.VMEM((OUT_SUBTILES, TM_OUT, D_MODEL), BF16)],
        compiler_params=pltpu.CompilerParams(dimension_semantics=("arbitrary",),
                                             vmem_limit_bytes=VMEM_LIMIT),
        name="mix_out",
    )(x3, yac3, t, tabs["cphi"], tabs["sphi"], tabs["cth"], tabs["sth"], wout, w1, w3, w2, gfin)
    return out.reshape(SEQ, D_MODEL)


def kernel(x, g_mix, w_in, w_conv, w_fourier, w_pool, pool_scale, w_out, g_ffn, w1, w3, w2, g_final):
    tabs = {k: jnp.asarray(v) for k, v in _tables().items()}
    block_diag = jax.scipy.linalg.block_diag
    wf_bd = jnp.stack([block_diag(*[w_fourier[l, h] for h in range(FOURIER_HEADS)]) for l in range(DEPTH)])
    wp_bd = jnp.stack([block_diag(*[w_pool[l, g] for g in range(len(POOL_WINDOWS))]) for l in range(DEPTH)])
    gf_col = g_ffn.reshape(DEPTH, D_MODEL, 1)
    wout, win = _fold_weights(tabs["chan"], wf_bd, wp_bd, pool_scale, w_out, w_in, g_mix)
    f_stage1 = tabs["f_stage1"].astype(BF16)

    xs = x.reshape(SEQ, D_MODEL)
    for l in range(DEPTH):
        yac, z, w1b, w3b, w2b = _mix_in(l, xs, win, w_conv, gf_col, w1, w3, w2)
        t = _dft_a(f_stage1, z)
        xs = _mix_out(l, xs, yac, t, tabs, wout, w1b, w3b, w2b, g_final[None, :],
                      final_norm=(l == DEPTH - 1))
    return xs.reshape(1, SEQ, D_MODEL)
```

```python
import functools

import numpy as np
import jax
import jax.numpy as jnp
from jax import lax
from jax.experimental import pallas as pl
from jax.experimental.pallas import tpu as pltpu

D_MODEL = 1024
SEQ = 16384
DEPTH = 2
HEAD_DIM = 64
CONV_DIM = 384
FOURIER_HEADS = 4
FOURIER_DIM = 256
POOL_WINDOWS = (2, 4, 8, 16)
POOL_GROUP_DIM = 96
POOL_DIM = 384
IN_PROJ_DIM = 3 * CONV_DIM + FOURIER_DIM + POOL_DIM
D_FF = 2816
EPS = 1e-6

RADIX = 128
YAC_DIM = CONV_DIM + POOL_DIM
Z_DIM = 2 * FOURIER_DIM
LANES = 128
SUBLANES = 8
DFT_COLS = 16

TM_IN = 1024
HALO = 8
assert all(w // 2 == 2 ** k for k, w in enumerate(POOL_WINDOWS)) and POOL_WINDOWS[-1] // 2 <= HALO
OUT_SUBTILES = 2
D_ROWS = RADIX // OUT_SUBTILES
TM_OUT = D_ROWS * SUBLANES
MXU_DIM = 256
FF_CHUNKS = (1536, 1280)
assert sum(FF_CHUNKS) == D_FF and all(w % MXU_DIM == 0 for w in FF_CHUNKS)
VMEM_LIMIT = 56 * 1024 * 1024

F32 = jnp.float32
BF16 = jnp.bfloat16


def _dot(a, b):
    return jnp.dot(a, b, preferred_element_type=F32)


def _unit_rms(v):
    ms = jnp.mean(v * v, axis=-1, keepdims=True)
    return v * lax.rsqrt(ms + EPS)


def _tables():
    r = np.arange(RADIX, dtype=np.float64)
    ang128 = 2.0 * np.pi * np.outer(r, r) / RADIX
    angs = 2.0 * np.pi * np.outer(r, r) / SEQ
    f_stage1 = np.concatenate([np.cos(ang128), -np.sin(ang128)], axis=0)
    e = np.arange(HEAD_DIM, dtype=np.float64)
    angc = 2.0 * np.pi * np.outer(e, e) / HEAD_DIM
    norm = 1.0 / np.sqrt(float(SEQ) * HEAD_DIM)
    eye = np.eye(FOURIER_HEADS)
    cbd = np.kron(eye, np.cos(angc)) * norm
    sbd = np.kron(eye, -np.sin(angc)) * norm
    return dict(
        f_stage1=f_stage1.astype(np.float32),
        cphi=np.cos(ang128).astype(np.float32), sphi=np.sin(ang128).astype(np.float32),
        cth=np.cos(angs).astype(np.float32), sth=np.sin(angs).astype(np.float32),
        chan=np.stack([cbd, sbd]).astype(np.float32),
    )


WIN_CH = 0
WIN_P = 2 * CONV_DIM
WIN_B = 2 * CONV_DIM + POOL_DIM
WIN_F = 3 * CONV_DIM + POOL_DIM
WIN_DIM = WIN_F + Z_DIM
assert WIN_DIM % (2 * MXU_DIM) == 0
FOLD_PARTS = 4
FOLD_ROWS = D_MODEL // FOLD_PARTS


def _dot_3pass(a, b):
    a_hi = a.astype(BF16)
    b_hi = b.astype(BF16)
    a_lo = (a - a_hi.astype(F32)).astype(BF16)
    b_lo = (b - b_hi.astype(F32)).astype(BF16)
    return _dot(a_hi, b_hi) + (_dot(a_hi, b_lo) + _dot(a_lo, b_hi))


def _row_as_column(row):
    diagonal = (lax.broadcasted_iota(jnp.int32, (LANES, LANES), 0)
                == lax.broadcasted_iota(jnp.int32, (LANES, LANES), 1))
    blocks = [jnp.sum(jnp.where(diagonal, row[:, k:k + LANES], 0.0), axis=1, keepdims=True)
              for k in range(0, row.shape[1], LANES)]
    return jnp.concatenate(blocks, axis=0)


def _fold_kernel(chan_ref, wf_ref, wp_ref, ps_ref, wout_ref, win_ref, gmix_ref, woe_ref, wine_ref,
                 gain_ref, ab_ref):
    layer = pl.program_id(0)
    part = pl.program_id(1)

    @pl.when(part == 0)
    def _():
        gain_ref[...] = _row_as_column(gmix_ref[pl.ds(layer, 1), :])
        wf = wf_ref[...]
        ab_ref[:, 0:FOURIER_DIM] = _dot_3pass(chan_ref[0], wf)
        ab_ref[:, FOURIER_DIM:Z_DIM] = _dot_3pass(chan_ref[1], wf)

    mixed = CONV_DIM + FOURIER_DIM
    woe_ref[0:mixed, :] = wout_ref[0:mixed, :].astype(BF16)
    woe_ref[mixed:D_MODEL, :] = _dot_3pass(wp_ref[...] * ps_ref[pl.ds(layer, 1), :],
                                           wout_ref[mixed:D_MODEL, :]).astype(BF16)
    c0 = CONV_DIM
    gain = gain_ref[pl.ds(pl.multiple_of(part * FOLD_ROWS, FOLD_ROWS), FOLD_ROWS), :]
    wine_ref[:, WIN_CH:WIN_P] = (win_ref[:, c0:3 * c0] * gain).astype(BF16)
    wine_ref[:, WIN_P:WIN_B] = (win_ref[:, 3 * c0 + FOURIER_DIM:IN_PROJ_DIM] * gain).astype(BF16)
    wine_ref[:, WIN_B:WIN_F] = (win_ref[:, 0:c0] * gain).astype(BF16)
    wfour = win_ref[:, 3 * c0:3 * c0 + FOURIER_DIM] * gain
    wine_ref[:, WIN_F:WIN_DIM] = _dot_3pass(wfour, ab_ref[...]).astype(BF16)


def _fold_weights(chan, wf_bd, wp_bd, pool_scale, w_out, w_in, g_mix):
    lay = lambda l, q: (l, 0, 0)
    whole = lambda l, q: (0, 0)
    rows = lambda l, q: (l, q, 0)
    cols = lambda l, q: (l, 0, q)
    return pl.pallas_call(
        _fold_kernel,
        grid=(DEPTH, FOLD_PARTS),
        in_specs=[pl.BlockSpec((2, FOURIER_DIM, FOURIER_DIM), lambda l, q: (0, 0, 0)),
                  pl.BlockSpec((None, FOURIER_DIM, FOURIER_DIM), lay),
                  pl.BlockSpec((None, POOL_DIM, POOL_DIM), lay),
                  pl.BlockSpec((DEPTH, POOL_DIM), whole),
                  pl.BlockSpec((None, D_MODEL, FOLD_ROWS), cols),
                  pl.BlockSpec((None, FOLD_ROWS, IN_PROJ_DIM), rows),
                  pl.BlockSpec((DEPTH, D_MODEL), whole)],
        out_specs=[pl.BlockSpec((None, D_MODEL, FOLD_ROWS), cols),
                   pl.BlockSpec((None, FOLD_ROWS, WIN_DIM), rows)],
        out_shape=[jax.ShapeDtypeStruct((DEPTH, D_MODEL, D_MODEL), BF16),
                   jax.ShapeDtypeStruct((DEPTH, D_MODEL, WIN_DIM), BF16)],
        scratch_shapes=[pltpu.VMEM((D_MODEL, 1), F32), pltpu.VMEM((FOURIER_DIM, Z_DIM), F32)],
        compiler_params=pltpu.CompilerParams(dimension_semantics=("arbitrary", "arbitrary")),
        name="fold_weights",
    )(chan, wf_bd, wp_bd, pool_scale, w_out, w_in, g_mix)


def _window_sums(p, rows, max_half):
    sums = {1: p + pltpu.roll(p, 1, 0)}
    h = 1
    while h < max_half:
        sums[2 * h] = pltpu.roll(sums[h], h, 0) + pltpu.roll(sums[h], rows - h, 0)
        h *= 2
    return sums


def _mix_in_kernel(x_ref, xprev_ref, xnext_ref, win_ref, wconv_ref, gffn_ref, w1_ref, w3_ref, w2_ref,
                   yac_ref, z_ref, w1b_ref, w3b_ref, w2b_ref, xe_ref):
    gffn = gffn_ref[...]
    w1b_ref[...] = (w1_ref[...] * gffn).astype(BF16)
    w3b_ref[...] = (w3_ref[...] * gffn).astype(BF16)
    w2b_ref[...] = w2_ref[...].astype(BF16)

    i = pl.program_id(0)
    n = pl.num_programs(0)
    rows = TM_IN + 2 * HALO
    prev = jnp.where(i > 0, _unit_rms(xprev_ref[...]), 0.0)
    nxt = jnp.where(i < n - 1, _unit_rms(xnext_ref[...]), 0.0)
    xe_ref[0:TM_IN, :] = _unit_rms(x_ref[...]).astype(BF16)
    xe_ref[TM_IN:rows, :] = jnp.concatenate([nxt, prev], axis=0).astype(BF16)

    z = _dot(xe_ref[...], win_ref[...])
    c0 = CONV_DIM
    tile = slice(0, TM_IN)
    zp = z[:, WIN_P:WIN_B]
    z_ref[...] = z[tile, WIN_F:WIN_DIM].astype(BF16)

    u = z[:, WIN_CH:WIN_CH + c0] * z[:, WIN_CH + c0:WIN_P]
    conv = (wconv_ref[0:1, :] * pltpu.roll(u, 1, 0) + wconv_ref[1:2, :] * u
            + wconv_ref[2:3, :] * pltpu.roll(u, rows - 1, 0))
    yac_ref[:, 0:c0] = z[tile, WIN_B:WIN_F] * conv[tile]

    lane_tiles = []
    for lo in range(0, POOL_DIM, LANES):
        p = zp[:, lo:lo + LANES]
        groups = range(lo // POOL_GROUP_DIM, (lo + LANES - 1) // POOL_GROUP_DIM + 1)
        sums = _window_sums(p, rows, POOL_WINDOWS[groups[-1]] // 2)
        lane = lo + lax.broadcasted_iota(jnp.int32, (1, LANES), 1)
        wsum = sums[POOL_WINDOWS[groups[-1]] // 2][tile]
        half = jnp.full((1, LANES), POOL_WINDOWS[groups[-1]] // 2, jnp.int32)
        for grp in reversed(groups[:-1]):
            inside = lane < (grp + 1) * POOL_GROUP_DIM
            wsum = jnp.where(inside, sums[POOL_WINDOWS[grp] // 2][tile], wsum)
            half = jnp.where(inside, POOL_WINDOWS[grp] // 2, half)
        cols = slice(c0 + lo, c0 + lo + LANES)
        yac_ref[:, cols] = wsum * (1.0 / (2 * half).astype(F32)) - p[tile]
        lane_tiles.append((cols, wsum, half, p))

    def clipped_window_rows(local, first_t):
        t = first_t + lax.broadcasted_iota(jnp.int32, (HALO, LANES), 0)
        for cols, wsum, half, p in lane_tiles:
            cnt = (jnp.minimum(t + half, SEQ) - jnp.maximum(t - half, 0)).astype(F32)
            yac_ref[local, cols] = wsum[local] / cnt - p[local]

    @pl.when(i == 0)
    def _():
        clipped_window_rows(slice(0, HALO), 0)

    @pl.when(i == n - 1)
    def _():
        clipped_window_rows(slice(TM_IN - HALO, TM_IN), SEQ - HALO)


def _mix_in(layer, x, win, wconv, gffn_col, w1, w3, w2):
    n = SEQ // TM_IN
    hb = TM_IN // HALO
    lay = lambda i: (layer, 0, 0)
    w_rows = lambda i: (layer, i, 0)
    cast_rows = lambda i: (i, 0)
    return pl.pallas_call(
        _mix_in_kernel,
        grid=(n,),
        in_specs=[
            pl.BlockSpec((TM_IN, D_MODEL), lambda i: (i, 0)),
            pl.BlockSpec((HALO, D_MODEL), lambda i: (jnp.maximum(i * hb - 1, 0), 0)),
            pl.BlockSpec((HALO, D_MODEL), lambda i: (jnp.minimum((i + 1) * hb, SEQ // HALO - 1), 0)),
            pl.BlockSpec((None, D_MODEL, WIN_DIM), lay, pipeline_mode=pl.Buffered(1)),
            pl.BlockSpec((None, 3, CONV_DIM), lay),
            pl.BlockSpec((None, D_MODEL // n, 1), w_rows),
            pl.BlockSpec((None, D_MODEL // n, D_FF), w_rows),
            pl.BlockSpec((None, D_MODEL // n, D_FF), w_rows),
            pl.BlockSpec((None, D_FF // n, D_MODEL), w_rows),
        ],
        out_specs=[pl.BlockSpec((TM_IN, YAC_DIM), lambda i: (i, 0)),
                   pl.BlockSpec((TM_IN, Z_DIM), lambda i: (i, 0)),
                   pl.BlockSpec((D_MODEL // n, D_FF), cast_rows),
                   pl.BlockSpec((D_MODEL // n, D_FF), cast_rows),
                   pl.BlockSpec((D_FF // n, D_MODEL), cast_rows)],
        out_shape=[jax.ShapeDtypeStruct((SEQ, YAC_DIM), F32),
                   jax.ShapeDtypeStruct((SEQ, Z_DIM), BF16),
                   jax.ShapeDtypeStruct((D_MODEL, D_FF), BF16),
                   jax.ShapeDtypeStruct((D_MODEL, D_FF), BF16),
                   jax.ShapeDtypeStruct((D_FF, D_MODEL), BF16)],
        scratch_shapes=[pltpu.VMEM((TM_IN + 2 * HALO, D_MODEL), BF16)],
        compiler_params=pltpu.CompilerParams(dimension_semantics=("arbitrary",),
                                             vmem_limit_bytes=VMEM_LIMIT),
        name="mix_in",
    )(x, x, x, win, wconv, gffn_col, w1, w3, w2)


def _dft_a_kernel(f_ref, z_ref, t_ref, zslab_ref, tslab_ref):
    h = FOURIER_DIM
    f = f_ref[...]
    zslab_ref[...] = jnp.swapaxes(z_ref[...].astype(F32), 0, 1)
    for j in range(DFT_COLS):
        res = _dot(f, zslab_ref[j].astype(BF16))
        tslab_ref[j, :, 0:h] = res[0:RADIX, 0:h] - res[RADIX:2 * RADIX, h:Z_DIM]
        tslab_ref[j, :, h:Z_DIM] = res[0:RADIX, h:Z_DIM] + res[RADIX:2 * RADIX, 0:h]
    t_ref[...] = jnp.swapaxes(tslab_ref[...], 0, 1).astype(BF16)


def _dft_a(f_stage1, z):
    z3 = z.reshape(RADIX, RADIX, Z_DIM)
    block = (RADIX, DFT_COLS, Z_DIM)
    return pl.pallas_call(
        _dft_a_kernel,
        grid=(RADIX // DFT_COLS,),
        in_specs=[pl.BlockSpec((2 * RADIX, RADIX), lambda t: (0, 0)),
                  pl.BlockSpec(block, lambda t: (0, t, 0))],
        out_specs=pl.BlockSpec(block, lambda t: (0, t, 0)),
        out_shape=jax.ShapeDtypeStruct((RADIX, RADIX, Z_DIM), BF16),
        scratch_shapes=[pltpu.VMEM((DFT_COLS, RADIX, Z_DIM), F32), pltpu.VMEM((DFT_COLS, RADIX, Z_DIM), F32)],
        compiler_params=pltpu.CompilerParams(dimension_semantics=("arbitrary",)),
        name="dft_a",
    )(f_stage1, z3)


def _mix_out_kernel(x_ref, yac_ref, t_ref, cphi_ref, sphi_ref, cth_ref, sth_ref, wout_ref,
                    w1_ref, w3_ref, w2_ref, gfin_ref, out_ref, yb_ref, ybuf, *, final_norm):
    chunk = pl.program_id(0)
    c0 = CONV_DIM
    h = FOURIER_DIM

    for sub in range(OUT_SUBTILES):
        local = slice(sub * D_ROWS, (sub + 1) * D_ROWS)
        cphi = cphi_ref[local, :]
        sphi = sphi_ref[local, :]
        for j in range(SUBLANES):
            c = chunk * SUBLANES + j
            cth = cth_ref[pl.ds(c, 1), :]
            sth = sth_ref[pl.ds(c, 1), :]
            mcos = (cphi * cth - sphi * sth).astype(BF16)
            msin = (sphi * cth + cphi * sth).astype(BF16)
            tj = t_ref[j]
            yb_ref[sub, j] = _dot(jnp.concatenate([mcos, msin], axis=1),
                                  jnp.concatenate([tj[:, 0:h], tj[:, h:Z_DIM]], axis=0))

        yac = yac_ref[local].reshape(TM_OUT, YAC_DIM)
        yb = jnp.swapaxes(yb_ref[sub], 0, 1).reshape(TM_OUT, h)
        ybuf[sub, :, 0:c0] = yac[:, 0:c0].astype(BF16)
        ybuf[sub, :, c0:c0 + h] = yb.astype(BF16)
        ybuf[sub, :, c0 + h:D_MODEL] = yac[:, c0:YAC_DIM].astype(BF16)

        x1 = x_ref[local].reshape(TM_OUT, D_MODEL) + _dot(ybuf[sub], wout_ref[...])
        xg = x1.astype(BF16)
        r = lax.rsqrt(jnp.mean(x1 * x1, axis=-1, keepdims=True) + EPS)
        acc = None
        start = 0
        for width in FF_CHUNKS:
            cols = slice(start, start + width)
            start += width
            gate = _dot(xg, w1_ref[:, cols]) * r
            up = _dot(xg, w3_ref[:, cols])
            hid = (gate * (1.0 / (1.0 + jnp.exp(-gate))) * up).astype(BF16)
            ffn = _dot(hid, w2_ref[cols, :])
            acc = ffn if acc is None else acc + ffn
        x2 = x1 + acc * r
        if final_norm:
            x2 = _unit_rms(x2) * gfin_ref[...]
        out_ref[local] = x2.reshape(D_ROWS, SUBLANES, D_MODEL)


def _mix_out(layer, x, yac, t, tabs, wout, w1, w3, w2, gfin, final_norm):
    x3 = x.reshape(RADIX, RADIX, D_MODEL)
    yac3 = yac.reshape(RADIX, RADIX, YAC_DIM)
    const = lambda s: (0, 0)
    lay = lambda s: (layer, 0, 0)
    tile = lambda s: (0, s, 0)
    once = dict(pipeline_mode=pl.Buffered(1))
    out = pl.pallas_call(
        functools.partial(_mix_out_kernel, final_norm=final_norm),
        grid=(RADIX // SUBLANES,),
        in_specs=[
            pl.BlockSpec((RADIX, SUBLANES, D_MODEL), tile),
            pl.BlockSpec((RADIX, SUBLANES, YAC_DIM), tile),
            pl.BlockSpec((SUBLANES, RADIX, Z_DIM), lambda s: (s, 0, 0)),
            pl.BlockSpec((RADIX, RADIX), const),
            pl.BlockSpec((RADIX, RADIX), const),
            pl.BlockSpec((RADIX, RADIX), const),
            pl.BlockSpec((RADIX, RADIX), const),
            pl.BlockSpec((None, D_MODEL, D_MODEL), lay, **once),
            pl.BlockSpec((D_MODEL, D_FF), const, **once),
            pl.BlockSpec((D_MODEL, D_FF), const, **once),
            pl.BlockSpec((D_FF, D_MODEL), const, **once),
            pl.BlockSpec((1, D_MODEL), const),
        ],
        out_specs=pl.BlockSpec((RADIX, SUBLANES, D_MODEL), tile),
        out_shape=jax.ShapeDtypeStruct((RADIX, RADIX, D_MODEL), F32),
        scratch_shapes=[pltpu.VMEM((OUT_SUBTILES, SUBLANES, D_ROWS, FOURIER_DIM), F32),
                        pltpu.VMEM((OUT_SUBTILES, TM_OUT, D_MODEL), BF16)],
        compiler_params=pltpu.CompilerParams(dimension_semantics=("arbitrary",),
                                             vmem_limit_bytes=VMEM_LIMIT),
        name="mix_out",
    )(x3, yac3, t, tabs["cphi"], tabs["sphi"], tabs["cth"], tabs["sth"], wout, w1, w3, w2, gfin)
    return out.reshape(SEQ, D_MODEL)


def kernel(x, g_mix, w_in, w_conv, w_fourier, w_pool, pool_scale, w_out, g_ffn, w1, w3, w2, g_final):
    tabs = {k: jnp.asarray(v) for k, v in _tables().items()}
    block_diag = jax.scipy.linalg.block_diag
    wf_bd = jnp.stack([block_diag(*[w_fourier[l, h] for h in range(FOURIER_HEADS)]) for l in range(DEPTH)])
    wp_bd = jnp.stack([block_diag(*[w_pool[l, g] for g in range(len(POOL_WINDOWS))]) for l in range(DEPTH)])
    gf_col = g_ffn.reshape(DEPTH, D_MODEL, 1)
    wout, win = _fold_weights(tabs["chan"], wf_bd, wp_bd, pool_scale, w_out, w_in, g_mix)
    f_stage1 = tabs["f_stage1"].astype(BF16)

    xs = x.reshape(SEQ, D_MODEL)
    for l in range(DEPTH):
        yac, z, w1b, w3b, w2b = _mix_in(l, xs, win, w_conv, gf_col, w1, w3, w2)
        t = _dft_a(f_stage1, z)
        xs = _mix_out(l, xs, yac, t, tabs, wout, w1b, w3b, w2b, g_final[None, :],
                      final_norm=(l == DEPTH - 1))
    return xs.reshape(1, SEQ, D_MODEL)
```

```python
import functools

import numpy as np
import jax
import jax.numpy as jnp
from jax import lax
from jax.experimental import pallas as pl
from jax.experimental.pallas import tpu as pltpu

D_MODEL = 1024
SEQ = 16384
DEPTH = 2
HEAD_DIM = 64
CONV_DIM = 384
FOURIER_HEADS = 4
FOURIER_DIM = 256
POOL_WINDOWS = (2, 4, 8, 16)
POOL_GROUP_DIM = 96
POOL_DIM = 384
IN_PROJ_DIM = 3 * CONV_DIM + FOURIER_DIM + POOL_DIM
D_FF = 2816
EPS = 1e-6

RADIX = 128
YAC_DIM = CONV_DIM + POOL_DIM
Z_DIM = 2 * FOURIER_DIM
LANES = 128
SUBLANES = 8
DFT_COLS = 16

TM_IN = 1024
HALO = 8
assert all(w // 2 == 2 ** k for k, w in enumerate(POOL_WINDOWS)) and POOL_WINDOWS[-1] // 2 <= HALO
OUT_SUBTILES = 2
D_ROWS = RADIX // OUT_SUBTILES
TM_OUT = D_ROWS * SUBLANES
MXU_DIM = 256
FF_CHUNKS = (1536, 1280)
assert sum(FF_CHUNKS) == D_FF and all(w % MXU_DIM == 0 for w in FF_CHUNKS)
VMEM_LIMIT = 56 * 1024 * 1024

F32 = jnp.float32
BF16 = jnp.bfloat16


def _dot(a, b):
    return jnp.dot(a, b, preferred_element_type=F32)


def _unit_rms(v):
    ms = jnp.mean(v * v, axis=-1, keepdims=True)
    return v * lax.rsqrt(ms + EPS)


def _tables():
    r = np.arange(RADIX, dtype=np.float64)
    ang128 = 2.0 * np.pi * np.outer(r, r) / RADIX
    angs = 2.0 * np.pi * np.outer(r, r) / SEQ
    f_stage1 = np.concatenate([np.cos(ang128), -np.sin(ang128)], axis=0)
    e = np.arange(HEAD_DIM, dtype=np.float64)
    angc = 2.0 * np.pi * np.outer(e, e) / HEAD_DIM
    norm = 1.0 / np.sqrt(float(SEQ) * HEAD_DIM)
    eye = np.eye(FOURIER_HEADS)
    cbd = np.kron(eye, np.cos(angc)) * norm
    sbd = np.kron(eye, -np.sin(angc)) * norm
    return dict(
        f_stage1=f_stage1.astype(np.float32),
        cphi=np.cos(ang128).astype(np.float32), sphi=np.sin(ang128).astype(np.float32),
        cth=np.cos(angs).astype(np.float32), sth=np.sin(angs).astype(np.float32),
        chan=np.stack([cbd, sbd]).astype(np.float32),
    )


WIN_CH = 0
WIN_P = 2 * CONV_DIM
WIN_B = 2 * CONV_DIM + POOL_DIM
WIN_F = 3 * CONV_DIM + POOL_DIM
WIN_DIM = WIN_F + Z_DIM
assert WIN_DIM % (2 * MXU_DIM) == 0
FOLD_PARTS = 4
FOLD_ROWS = D_MODEL // FOLD_PARTS


def _dot_3pass(a, b):
    a_hi = a.astype(BF16)
    b_hi = b.astype(BF16)
    a_lo = (a - a_hi.astype(F32)).astype(BF16)
    b_lo = (b - b_hi.astype(F32)).astype(BF16)
    return _dot(a_hi, b_hi) + (_dot(a_hi, b_lo) + _dot(a_lo, b_hi))


def _row_as_column(row):
    diagonal = (lax.broadcasted_iota(jnp.int32, (LANES, LANES), 0)
                == lax.broadcasted_iota(jnp.int32, (LANES, LANES), 1))
    blocks = [jnp.sum(jnp.where(diagonal, row[:, k:k + LANES], 0.0), axis=1, keepdims=True)
              for k in range(0, row.shape[1], LANES)]
    return jnp.concatenate(blocks, axis=0)


def _fold_kernel(chan_ref, wf_ref, wp_ref, ps_ref, wout_ref, win_ref, gmix_ref, woe_ref, wine_ref,
                 gain_ref, ab_ref):
    layer = pl.program_id(0)
    part = pl.program_id(1)

    @pl.when(part == 0)
    def _():
        gain_ref[...] = _row_as_column(gmix_ref[pl.ds(layer, 1), :])
        wf = wf_ref[...]
        ab_ref[:, 0:FOURIER_DIM] = _dot_3pass(chan_ref[0], wf)
        ab_ref[:, FOURIER_DIM:Z_DIM] = _dot_3pass(chan_ref[1], wf)

    mixed = CONV_DIM + FOURIER_DIM
    woe_ref[0:mixed, :] = wout_ref[0:mixed, :].astype(BF16)
    woe_ref[mixed:D_MODEL, :] = _dot_3pass(wp_ref[...] * ps_ref[pl.ds(layer, 1), :],
                                           wout_ref[mixed:D_MODEL, :]).astype(BF16)
    c0 = CONV_DIM
    gain = gain_ref[pl.ds(pl.multiple_of(part * FOLD_ROWS, FOLD_ROWS), FOLD_ROWS), :]
    wine_ref[:, WIN_CH:WIN_P] = (win_ref[:, c0:3 * c0] * gain).astype(BF16)
    wine_ref[:, WIN_P:WIN_B] = (win_ref[:, 3 * c0 + FOURIER_DIM:IN_PROJ_DIM] * gain).astype(BF16)
    wine_ref[:, WIN_B:WIN_F] = (win_ref[:, 0:c0] * gain).astype(BF16)
    wfour = win_ref[:, 3 * c0:3 * c0 + FOURIER_DIM] * gain
    wine_ref[:, WIN_F:WIN_DIM] = _dot_3pass(wfour, ab_ref[...]).astype(BF16)


def _fold_weights(chan, wf_bd, wp_bd, pool_scale, w_out, w_in, g_mix):
    lay = lambda l, q: (l, 0, 0)
    whole = lambda l, q: (0, 0)
    rows = lambda l, q: (l, q, 0)
    cols = lambda l, q: (l, 0, q)
    return pl.pallas_call(
        _fold_kernel,
        grid=(DEPTH, FOLD_PARTS),
        in_specs=[pl.BlockSpec((2, FOURIER_DIM, FOURIER_DIM), lambda l, q: (0, 0, 0)),
                  pl.BlockSpec((None, FOURIER_DIM, FOURIER_DIM), lay),
                  pl.BlockSpec((None, POOL_DIM, POOL_DIM), lay),
                  pl.BlockSpec((DEPTH, POOL_DIM), whole),
                  pl.BlockSpec((None, D_MODEL, FOLD_ROWS), cols),
                  pl.BlockSpec((None, FOLD_ROWS, IN_PROJ_DIM), rows),
                  pl.BlockSpec((DEPTH, D_MODEL), whole)],
        out_specs=[pl.BlockSpec((None, D_MODEL, FOLD_ROWS), cols),
                   pl.BlockSpec((None, FOLD_ROWS, WIN_DIM), rows)],
        out_shape=[jax.ShapeDtypeStruct((DEPTH, D_MODEL, D_MODEL), BF16),
                   jax.ShapeDtypeStruct((DEPTH, D_MODEL, WIN_DIM), BF16)],
        scratch_shapes=[pltpu.VMEM((D_MODEL, 1), F32), pltpu.VMEM((FOURIER_DIM, Z_DIM), F32)],
        compiler_params=pltpu.CompilerParams(dimension_semantics=("arbitrary", "arbitrary")),
        name="fold_weights",
    )(chan, wf_bd, wp_bd, pool_scale, w_out, w_in, g_mix)


def _window_sums(p, rows, max_half):
    sums = {1: p + pltpu.roll(p, 1, 0)}
    h = 1
    while h < max_half:
        sums[2 * h] = pltpu.roll(sums[h], h, 0) + pltpu.roll(sums[h], rows - h, 0)
        h *= 2
    return sums


def _mix_in_kernel(x_ref, xprev_ref, xnext_ref, win_ref, wconv_ref, gffn_ref, w1_ref, w3_ref, w2_ref,
                   yac_ref, z_ref, w1b_ref, w3b_ref, w2b_ref, xe_ref):
    gffn = gffn_ref[...]
    w1b_ref[...] = (w1_ref[...] * gffn).astype(BF16)
    w3b_ref[...] = (w3_ref[...] * gffn).astype(BF16)
    w2b_ref[...] = w2_ref[...].astype(BF16)

    i = pl.program_id(0)
    n = pl.num_programs(0)
    rows = TM_IN + 2 * HALO
    prev = jnp.where(i > 0, _unit_rms(xprev_ref[...]), 0.0)
    nxt = jnp.where(i < n - 1, _unit_rms(xnext_ref[...]), 0.0)
    xe_ref[0:TM_IN, :] = _unit_rms(x_ref[...]).astype(BF16)
    xe_ref[TM_IN:rows, :] = jnp.concatenate([nxt, prev], axis=0).astype(BF16)

    z = _dot(xe_ref[...], win_ref[...])
    c0 = CONV_DIM
    tile = slice(0, TM_IN)
    zp = z[:, WIN_P:WIN_B]
    z_ref[...] = z[tile, WIN_F:WIN_DIM].astype(BF16)

    u = z[:, WIN_CH:WIN_CH + c0] * z[:, WIN_CH + c0:WIN_P]
    conv = (wconv_ref[0:1, :] * pltpu.roll(u, 1, 0) + wconv_ref[1:2, :] * u
            + wconv_ref[2:3, :] * pltpu.roll(u, rows - 1, 0))
    yac_ref[:, 0:c0] = z[tile, WIN_B:WIN_F] * conv[tile]

    lane_tiles = []
    for lo in range(0, POOL_DIM, LANES):
        p = zp[:, lo:lo + LANES]
        groups = range(lo // POOL_GROUP_DIM, (lo + LANES - 1) // POOL_GROUP_DIM + 1)
        sums = _window_sums(p, rows, POOL_WINDOWS[groups[-1]] // 2)
        lane = lo + lax.broadcasted_iota(jnp.int32, (1, LANES), 1)
        wsum = sums[POOL_WINDOWS[groups[-1]] // 2][tile]
        half = jnp.full((1, LANES), POOL_WINDOWS[groups[-1]] // 2, jnp.int32)
        for grp in reversed(groups[:-1]):
            inside = lane < (grp + 1) * POOL_GROUP_DIM
            wsum = jnp.where(inside, sums[POOL_WINDOWS[grp] // 2][tile], wsum)
            half = jnp.where(inside, POOL_WINDOWS[grp] // 2, half)
        cols = slice(c0 + lo, c0 + lo + LANES)
        yac_ref[:, cols] = wsum * (1.0 / (2 * half).astype(F32)) - p[tile]
        lane_tiles.append((cols, wsum, half, p))

    def clipped_window_rows(local, first_t):
        t = first_t + lax.broadcasted_iota(jnp.int32, (HALO, LANES), 0)
        for cols, wsum, half, p in lane_tiles:
            cnt = (jnp.minimum(t + half, SEQ) - jnp.maximum(t - half, 0)).astype(F32)
            yac_ref[local, cols] = wsum[local] / cnt - p[local]

    @pl.when(i == 0)
    def _():
        clipped_window_rows(slice(0, HALO), 0)

    @pl.when(i == n - 1)
    def _():
        clipped_window_rows(slice(TM_IN - HALO, TM_IN), SEQ - HALO)


def _mix_in(layer, x, win, wconv, gffn_col, w1, w3, w2):
    n = SEQ // TM_IN
    hb = TM_IN // HALO
    lay = lambda i: (layer, 0, 0)
    w_rows = lambda i: (layer, i, 0)
    cast_rows = lambda i: (i, 0)
    return pl.pallas_call(
        _mix_in_kernel,
        grid=(n,),
        in_specs=[
            pl.BlockSpec((TM_IN, D_MODEL), lambda i: (i, 0)),
            pl.BlockSpec((HALO, D_MODEL), lambda i: (jnp.maximum(i * hb - 1, 0), 0)),
            pl.BlockSpec((HALO, D_MODEL), lambda i: (jnp.minimum((i + 1) * hb, SEQ // HALO - 1), 0)),
            pl.BlockSpec((None, D_MODEL, WIN_DIM), lay, pipeline_mode=pl.Buffered(1)),
            pl.BlockSpec((None, 3, CONV_DIM), lay),
            pl.BlockSpec((None, D_MODEL // n, 1), w_rows),
            pl.BlockSpec((None, D_MODEL // n, D_FF), w_rows),
            pl.BlockSpec((None, D_MODEL // n, D_FF), w_rows),
            pl.BlockSpec((None, D_FF // n, D_MODEL), w_rows),
        ],
        out_specs=[pl.BlockSpec((TM_IN, YAC_DIM), lambda i: (i, 0)),
                   pl.BlockSpec((TM_IN, Z_DIM), lambda i: (i, 0)),
                   pl.BlockSpec((D_MODEL // n, D_FF), cast_rows),
                   pl.BlockSpec((D_MODEL // n, D_FF), cast_rows),
                   pl.BlockSpec((D_FF // n, D_MODEL), cast_rows)],
        out_shape=[jax.ShapeDtypeStruct((SEQ, YAC_DIM), F32),
                   jax.ShapeDtypeStruct((SEQ, Z_DIM), BF16),
                   jax.ShapeDtypeStruct((D_MODEL, D_FF), BF16),
                   jax.ShapeDtypeStruct((D_MODEL, D_FF), BF16),
                   jax.ShapeDtypeStruct((D_FF, D_MODEL), BF16)],
        scratch_shapes=[pltpu.VMEM((TM_IN + 2 * HALO, D_MODEL), BF16)],
        compiler_params=pltpu.CompilerParams(dimension_semantics=("arbitrary",),
                                             vmem_limit_bytes=VMEM_LIMIT),
        name="mix_in",
    )(x, x, x, win, wconv, gffn_col, w1, w3, w2)


def _dft_a_kernel(f_ref, z_ref, t_ref, zslab_ref, tslab_ref):
    h = FOURIER_DIM
    f = f_ref[...]
    zslab_ref[...] = jnp.swapaxes(z_ref[...].astype(F32), 0, 1)
    for j in range(DFT_COLS):
        res = _dot(f, zslab_ref[j].astype(BF16))
        tslab_ref[j, :, 0:h] = res[0:RADIX, 0:h] - res[RADIX:2 * RADIX, h:Z_DIM]
        tslab_ref[j, :, h:Z_DIM] = res[0:RADIX, h:Z_DIM] + res[RADIX:2 * RADIX, 0:h]
    t_ref[...] = jnp.swapaxes(tslab_ref[...], 0, 1).astype(BF16)


def _dft_a(f_stage1, z):
    z3 = z.reshape(RADIX, RADIX, Z_DIM)
    block = (RADIX, DFT_COLS, Z_DIM)
    return pl.pallas_call(
        _dft_a_kernel,
        grid=(RADIX // DFT_COLS,),
        in_specs=[pl.BlockSpec((2 * RADIX, RADIX), lambda t: (0, 0)),
                  pl.BlockSpec(block, lambda t: (0, t, 0))],
        out_specs=pl.BlockSpec(block, lambda t: (0, t, 0)),
        out_shape=jax.ShapeDtypeStruct((RADIX, RADIX, Z_DIM), BF16),
        scratch_shapes=[pltpu.VMEM((DFT_COLS, RADIX, Z_DIM), F32), pltpu.VMEM((DFT_COLS, RADIX, Z_DIM), F32)],
        compiler_params=pltpu.CompilerParams(dimension_semantics=("arbitrary",)),
        name="dft_a",
    )(f_stage1, z3)


def _mix_out_kernel(x_ref, yac_ref, t_ref, cphi_ref, sphi_ref, cth_ref, sth_ref, wout_ref,
                    w1_ref, w3_ref, w2_ref, gfin_ref, out_ref, yb_ref, ybuf, *, final_norm):
    chunk = pl.program_id(0)
    c0 = CONV_DIM
    h = FOURIER_DIM

    for sub in range(OUT_SUBTILES):
        local = slice(sub * D_ROWS, (sub + 1) * D_ROWS)
        cphi = cphi_ref[local, :]
        sphi = sphi_ref[local, :]
        for j in range(SUBLANES):
            c = chunk * SUBLANES + j
            cth = cth_ref[pl.ds(c, 1), :]
            sth = sth_ref[pl.ds(c, 1), :]
            mcos = (cphi * cth - sphi * sth).astype(BF16)
            msin = (sphi * cth + cphi * sth).astype(BF16)
            tj = t_ref[j]
            yb_ref[sub, :, j, :] = _dot(jnp.concatenate([mcos, msin], axis=1),
                                        jnp.concatenate([tj[:, 0:h], tj[:, h:Z_DIM]], axis=0))

        yac = yac_ref[local].reshape(TM_OUT, YAC_DIM)
        yb = yb_ref[sub].reshape(TM_OUT, h)
        ybuf[sub, :, 0:c0] = yac[:, 0:c0].astype(BF16)
        ybuf[sub, :, c0:c0 + h] = yb.astype(BF16)
        ybuf[sub, :, c0 + h:D_MODEL] = yac[:, c0:YAC_DIM].astype(BF16)

        x1 = x_ref[local].reshape(TM_OUT, D_MODEL) + _dot(ybuf[sub], wout_ref[...])
        xg = x1.astype(BF16)
        r = lax.rsqrt(jnp.mean(x1 * x1, axis=-1, keepdims=True) + EPS)
        acc = None
        start = 0
        for width in FF_CHUNKS:
            cols = slice(start, start + width)
            start += width
            gate = _dot(xg, w1_ref[:, cols]) * r
            up = _dot(xg, w3_ref[:, cols])
            hid = (gate * (1.0 / (1.0 + jnp.exp(-gate))) * up).astype(BF16)
            ffn = _dot(hid, w2_ref[cols, :])
            acc = ffn if acc is None else acc + ffn
        x2 = x1 + acc * r
        if final_norm:
            x2 = _unit_rms(x2) * gfin_ref[...]
        out_ref[local] = x2.reshape(D_ROWS, SUBLANES, D_MODEL)


def _mix_out(layer, x, yac, t, tabs, wout, w1, w3, w2, gfin, final_norm):
    x3 = x.reshape(RADIX, RADIX, D_MODEL)
    yac3 = yac.reshape(RADIX, RADIX, YAC_DIM)
    const = lambda s: (0, 0)
    lay = lambda s: (layer, 0, 0)
    tile = lambda s: (0, s, 0)
    once = dict(pipeline_mode=pl.Buffered(1))
    out = pl.pallas_call(
        functools.partial(_mix_out_kernel, final_norm=final_norm),
        grid=(RADIX // SUBLANES,),
        in_specs=[
            pl.BlockSpec((RADIX, SUBLANES, D_MODEL), tile),
            pl.BlockSpec((RADIX, SUBLANES, YAC_DIM), tile),
            pl.BlockSpec((SUBLANES, RADIX, Z_DIM), lambda s: (s, 0, 0)),
            pl.BlockSpec((RADIX, RADIX), const),
            pl.BlockSpec((RADIX, RADIX), const),
            pl.BlockSpec((RADIX, RADIX), const),
            pl.BlockSpec((RADIX, RADIX), const),
            pl.BlockSpec((None, D_MODEL, D_MODEL), lay, **once),
            pl.BlockSpec((D_MODEL, D_FF), const, **once),
            pl.BlockSpec((D_MODEL, D_FF), const, **once),
            pl.BlockSpec((D_FF, D_MODEL), const, **once),
            pl.BlockSpec((1, D_MODEL), const),
        ],
        out_specs=pl.BlockSpec((RADIX, SUBLANES, D_MODEL), tile),
        out_shape=jax.ShapeDtypeStruct((RADIX, RADIX, D_MODEL), F32),
        scratch_shapes=[pltpu.VMEM((OUT_SUBTILES, D_ROWS, SUBLANES, FOURIER_DIM), F32),
                        pltpu.VMEM((OUT_SUBTILES, TM_OUT, D_MODEL), BF16)],
        compiler_params=pltpu.CompilerParams(dimension_semantics=("arbitrary",),
                                             vmem_limit_bytes=VMEM_LIMIT),
        name="mix_out",
    )(x3, yac3, t, tabs["cphi"], tabs["sphi"], tabs["cth"], tabs["sth"], wout, w1, w3, w2, gfin)
    return out.reshape(SEQ, D_MODEL)


def kernel(x, g_mix, w_in, w_conv, w_fourier, w_pool, pool_scale, w_out, g_ffn, w1, w3, w2, g_final):
    tabs = {k: jnp.asarray(v) for k, v in _tables().items()}
    block_diag = jax.scipy.linalg.block_diag
    wf_bd = jnp.stack([block_diag(*[w_fourier[l, h] for h in range(FOURIER_HEADS)]) for l in range(DEPTH)])
    wp_bd = jnp.stack([block_diag(*[w_pool[l, g] for g in range(len(POOL_WINDOWS))]) for l in range(DEPTH)])
    gf_col = g_ffn.reshape(DEPTH, D_MODEL, 1)
    wout, win = _fold_weights(tabs["chan"], wf_bd, wp_bd, pool_scale, w_out, w_in, g_mix)
    f_stage1 = tabs["f_stage1"].astype(BF16)

    xs = x.reshape(SEQ, D_MODEL)
    for l in range(DEPTH):
        yac, z, w1b, w3b, w2b = _mix_in(l, xs, win, w_conv, gf_col, w1, w3, w2)
        t = _dft_a(f_stage1, z)
        xs = _mix_out(l, xs, yac, t, tabs, wout, w1b, w3b, w2b, g_final[None, :],
                      final_norm=(l == DEPTH - 1))
    return xs.reshape(1, SEQ, D_MODEL)
```

```python
import functools

import numpy as np
import jax
import jax.numpy as jnp
from jax import lax
from jax.experimental import pallas as pl
from jax.experimental.pallas import tpu as pltpu

D_MODEL = 1024
SEQ = 16384
DEPTH = 2
HEAD_DIM = 64
CONV_DIM = 384
FOURIER_HEADS = 4
FOURIER_DIM = 256
POOL_WINDOWS = (2, 4, 8, 16)
POOL_GROUP_DIM = 96
POOL_DIM = 384
IN_PROJ_DIM = 3 * CONV_DIM + FOURIER_DIM + POOL_DIM
D_FF = 2816
EPS = 1e-6

RADIX = 128
YAC_DIM = CONV_DIM + POOL_DIM
Z_DIM = 2 * FOURIER_DIM
LANES = 128
SUBLANES = 8
DFT_COLS = 32

TM_IN = 1024
HALO = 8
assert all(w // 2 == 2 ** k for k, w in enumerate(POOL_WINDOWS)) and POOL_WINDOWS[-1] // 2 <= HALO
OUT_SUBTILES = 2
D_ROWS = RADIX // OUT_SUBTILES
TM_OUT = D_ROWS * SUBLANES
MXU_DIM = 256
FF_CHUNKS = (1536, 1280)
assert sum(FF_CHUNKS) == D_FF and all(w % MXU_DIM == 0 for w in FF_CHUNKS)
VMEM_LIMIT = 56 * 1024 * 1024

F32 = jnp.float32
BF16 = jnp.bfloat16


def _dot(a, b):
    return jnp.dot(a, b, preferred_element_type=F32)


def _unit_rms(v):
    ms = jnp.mean(v * v, axis=-1, keepdims=True)
    return v * lax.rsqrt(ms + EPS)


def _tables():
    r = np.arange(RADIX, dtype=np.float64)
    ang128 = 2.0 * np.pi * np.outer(r, r) / RADIX
    angs = 2.0 * np.pi * np.outer(r, r) / SEQ
    f_stage1 = np.concatenate([np.cos(ang128), -np.sin(ang128)], axis=0)
    e = np.arange(HEAD_DIM, dtype=np.float64)
    angc = 2.0 * np.pi * np.outer(e, e) / HEAD_DIM
    norm = 1.0 / np.sqrt(float(SEQ) * HEAD_DIM)
    eye = np.eye(FOURIER_HEADS)
    cbd = np.kron(eye, np.cos(angc)) * norm
    sbd = np.kron(eye, -np.sin(angc)) * norm
    return dict(
        f_stage1=f_stage1.astype(np.float32),
        cphi=np.cos(ang128).astype(np.float32), sphi=np.sin(ang128).astype(np.float32),
        cth=np.cos(angs).astype(np.float32), sth=np.sin(angs).astype(np.float32),
        chan=np.stack([cbd, sbd]).astype(np.float32),
    )


WIN_CH = 0
WIN_P = 2 * CONV_DIM
WIN_B = 2 * CONV_DIM + POOL_DIM
WIN_F = 3 * CONV_DIM + POOL_DIM
WIN_DIM = WIN_F + Z_DIM
assert WIN_DIM % (2 * MXU_DIM) == 0
FOLD_PARTS = 4
FOLD_ROWS = D_MODEL // FOLD_PARTS


def _dot_3pass(a, b):
    a_hi = a.astype(BF16)
    b_hi = b.astype(BF16)
    a_lo = (a - a_hi.astype(F32)).astype(BF16)
    b_lo = (b - b_hi.astype(F32)).astype(BF16)
    return _dot(a_hi, b_hi) + (_dot(a_hi, b_lo) + _dot(a_lo, b_hi))


def _row_as_column(row):
    diagonal = (lax.broadcasted_iota(jnp.int32, (LANES, LANES), 0)
                == lax.broadcasted_iota(jnp.int32, (LANES, LANES), 1))
    blocks = [jnp.sum(jnp.where(diagonal, row[:, k:k + LANES], 0.0), axis=1, keepdims=True)
              for k in range(0, row.shape[1], LANES)]
    return jnp.concatenate(blocks, axis=0)


def _fold_kernel(chan_ref, wf_ref, wp_ref, ps_ref, wout_ref, win_ref, gmix_ref, woe_ref, wine_ref,
                 gain_ref, ab_ref):
    layer = pl.program_id(0)
    part = pl.program_id(1)

    @pl.when(part == 0)
    def _():
        gain_ref[...] = _row_as_column(gmix_ref[pl.ds(layer, 1), :])
        wf = wf_ref[...]
        ab_ref[:, 0:FOURIER_DIM] = _dot_3pass(chan_ref[0], wf)
        ab_ref[:, FOURIER_DIM:Z_DIM] = _dot_3pass(chan_ref[1], wf)

    mixed = CONV_DIM + FOURIER_DIM
    woe_ref[0:mixed, :] = wout_ref[0:mixed, :].astype(BF16)
    woe_ref[mixed:D_MODEL, :] = _dot_3pass(wp_ref[...] * ps_ref[pl.ds(layer, 1), :],
                                           wout_ref[mixed:D_MODEL, :]).astype(BF16)
    c0 = CONV_DIM
    gain = gain_ref[pl.ds(pl.multiple_of(part * FOLD_ROWS, FOLD_ROWS), FOLD_ROWS), :]
    wine_ref[:, WIN_CH:WIN_P] = (win_ref[:, c0:3 * c0] * gain).astype(BF16)
    wine_ref[:, WIN_P:WIN_B] = (win_ref[:, 3 * c0 + FOURIER_DIM:IN_PROJ_DIM] * gain).astype(BF16)
    wine_ref[:, WIN_B:WIN_F] = (win_ref[:, 0:c0] * gain).astype(BF16)
    wfour = win_ref[:, 3 * c0:3 * c0 + FOURIER_DIM] * gain
    wine_ref[:, WIN_F:WIN_DIM] = _dot_3pass(wfour, ab_ref[...]).astype(BF16)


def _fold_weights(chan, wf_bd, wp_bd, pool_scale, w_out, w_in, g_mix):
    lay = lambda l, q: (l, 0, 0)
    whole = lambda l, q: (0, 0)
    rows = lambda l, q: (l, q, 0)
    cols = lambda l, q: (l, 0, q)
    return pl.pallas_call(
        _fold_kernel,
        grid=(DEPTH, FOLD_PARTS),
        in_specs=[pl.BlockSpec((2, FOURIER_DIM, FOURIER_DIM), lambda l, q: (0, 0, 0)),
                  pl.BlockSpec((None, FOURIER_DIM, FOURIER_DIM), lay),
                  pl.BlockSpec((None, POOL_DIM, POOL_DIM), lay),
                  pl.BlockSpec((DEPTH, POOL_DIM), whole),
                  pl.BlockSpec((None, D_MODEL, FOLD_ROWS), cols),
                  pl.BlockSpec((None, FOLD_ROWS, IN_PROJ_DIM), rows),
                  pl.BlockSpec((DEPTH, D_MODEL), whole)],
        out_specs=[pl.BlockSpec((None, D_MODEL, FOLD_ROWS), cols),
                   pl.BlockSpec((None, FOLD_ROWS, WIN_DIM), rows)],
        out_shape=[jax.ShapeDtypeStruct((DEPTH, D_MODEL, D_MODEL), BF16),
                   jax.ShapeDtypeStruct((DEPTH, D_MODEL, WIN_DIM), BF16)],
        scratch_shapes=[pltpu.VMEM((D_MODEL, 1), F32), pltpu.VMEM((FOURIER_DIM, Z_DIM), F32)],
        compiler_params=pltpu.CompilerParams(dimension_semantics=("arbitrary", "arbitrary")),
        name="fold_weights",
    )(chan, wf_bd, wp_bd, pool_scale, w_out, w_in, g_mix)


def _window_sums(p, rows, max_half):
    sums = {1: p + pltpu.roll(p, 1, 0)}
    h = 1
    while h < max_half:
        sums[2 * h] = pltpu.roll(sums[h], h, 0) + pltpu.roll(sums[h], rows - h, 0)
        h *= 2
    return sums


def _mix_in_kernel(x_ref, xprev_ref, xnext_ref, win_ref, wconv_ref, gffn_ref, w1_ref, w3_ref, w2_ref,
                   yac_ref, z_ref, w1b_ref, w3b_ref, w2b_ref, xe_ref):
    gffn = gffn_ref[...]
    w1b_ref[...] = (w1_ref[...] * gffn).astype(BF16)
    w3b_ref[...] = (w3_ref[...] * gffn).astype(BF16)
    w2b_ref[...] = w2_ref[...].astype(BF16)

    i = pl.program_id(0)
    n = pl.num_programs(0)
    rows = TM_IN + 2 * HALO
    prev = jnp.where(i > 0, _unit_rms(xprev_ref[...]), 0.0)
    nxt = jnp.where(i < n - 1, _unit_rms(xnext_ref[...]), 0.0)
    xe_ref[0:TM_IN, :] = _unit_rms(x_ref[...]).astype(BF16)
    xe_ref[TM_IN:rows, :] = jnp.concatenate([nxt, prev], axis=0).astype(BF16)

    z = _dot(xe_ref[...], win_ref[...])
    c0 = CONV_DIM
    tile = slice(0, TM_IN)
    zp = z[:, WIN_P:WIN_B]
    z_ref[...] = z[tile, WIN_F:WIN_DIM].astype(BF16)

    u = z[:, WIN_CH:WIN_CH + c0] * z[:, WIN_CH + c0:WIN_P]
    conv = (wconv_ref[0:1, :] * pltpu.roll(u, 1, 0) + wconv_ref[1:2, :] * u
            + wconv_ref[2:3, :] * pltpu.roll(u, rows - 1, 0))
    yac_ref[:, 0:c0] = z[tile, WIN_B:WIN_F] * conv[tile]

    lane_tiles = []
    for lo in range(0, POOL_DIM, LANES):
        p = zp[:, lo:lo + LANES]
        groups = range(lo // POOL_GROUP_DIM, (lo + LANES - 1) // POOL_GROUP_DIM + 1)
        sums = _window_sums(p, rows, POOL_WINDOWS[groups[-1]] // 2)
        lane = lo + lax.broadcasted_iota(jnp.int32, (1, LANES), 1)
        wsum = sums[POOL_WINDOWS[groups[-1]] // 2][tile]
        half = jnp.full((1, LANES), POOL_WINDOWS[groups[-1]] // 2, jnp.int32)
        for grp in reversed(groups[:-1]):
            inside = lane < (grp + 1) * POOL_GROUP_DIM
            wsum = jnp.where(inside, sums[POOL_WINDOWS[grp] // 2][tile], wsum)
            half = jnp.where(inside, POOL_WINDOWS[grp] // 2, half)
        cols = slice(c0 + lo, c0 + lo + LANES)
        yac_ref[:, cols] = wsum * (1.0 / (2 * half).astype(F32)) - p[tile]
        lane_tiles.append((cols, wsum, half, p))

    def clipped_window_rows(local, first_t):
        t = first_t + lax.broadcasted_iota(jnp.int32, (HALO, LANES), 0)
        for cols, wsum, half, p in lane_tiles:
            cnt = (jnp.minimum(t + half, SEQ) - jnp.maximum(t - half, 0)).astype(F32)
            yac_ref[local, cols] = wsum[local] / cnt - p[local]

    @pl.when(i == 0)
    def _():
        clipped_window_rows(slice(0, HALO), 0)

    @pl.when(i == n - 1)
    def _():
        clipped_window_rows(slice(TM_IN - HALO, TM_IN), SEQ - HALO)


def _mix_in(layer, x, win, wconv, gffn_col, w1, w3, w2):
    n = SEQ // TM_IN
    hb = TM_IN // HALO
    lay = lambda i: (layer, 0, 0)
    w_rows = lambda i: (layer, i, 0)
    cast_rows = lambda i: (i, 0)
    return pl.pallas_call(
        _mix_in_kernel,
        grid=(n,),
        in_specs=[
            pl.BlockSpec((TM_IN, D_MODEL), lambda i: (i, 0)),
            pl.BlockSpec((HALO, D_MODEL), lambda i: (jnp.maximum(i * hb - 1, 0), 0)),
            pl.BlockSpec((HALO, D_MODEL), lambda i: (jnp.minimum((i + 1) * hb, SEQ // HALO - 1), 0)),
            pl.BlockSpec((None, D_MODEL, WIN_DIM), lay, pipeline_mode=pl.Buffered(1)),
            pl.BlockSpec((None, 3, CONV_DIM), lay),
            pl.BlockSpec((None, D_MODEL // n, 1), w_rows),
            pl.BlockSpec((None, D_MODEL // n, D_FF), w_rows),
            pl.BlockSpec((None, D_MODEL // n, D_FF), w_rows),
            pl.BlockSpec((None, D_FF // n, D_MODEL), w_rows),
        ],
        out_specs=[pl.BlockSpec((TM_IN, YAC_DIM), lambda i: (i, 0)),
                   pl.BlockSpec((TM_IN, Z_DIM), lambda i: (i, 0)),
                   pl.BlockSpec((D_MODEL // n, D_FF), cast_rows),
                   pl.BlockSpec((D_MODEL // n, D_FF), cast_rows),
                   pl.BlockSpec((D_FF // n, D_MODEL), cast_rows)],
        out_shape=[jax.ShapeDtypeStruct((SEQ, YAC_DIM), F32),
                   jax.ShapeDtypeStruct((SEQ, Z_DIM), BF16),
                   jax.ShapeDtypeStruct((D_MODEL, D_FF), BF16),
                   jax.ShapeDtypeStruct((D_MODEL, D_FF), BF16),
                   jax.ShapeDtypeStruct((D_FF, D_MODEL), BF16)],
        scratch_shapes=[pltpu.VMEM((TM_IN + 2 * HALO, D_MODEL), BF16)],
        compiler_params=pltpu.CompilerParams(dimension_semantics=("arbitrary",),
                                             vmem_limit_bytes=VMEM_LIMIT),
        name="mix_in",
    )(x, x, x, win, wconv, gffn_col, w1, w3, w2)


def _dft_a_kernel(f_ref, z_ref, t_ref, zslab_ref, tslab_ref):
    h = FOURIER_DIM
    f = f_ref[...]
    zslab_ref[...] = jnp.swapaxes(z_ref[...].astype(F32), 0, 1)
    for j in range(DFT_COLS):
        res = _dot(f, zslab_ref[j].astype(BF16))
        tslab_ref[j, :, 0:h] = res[0:RADIX, 0:h] - res[RADIX:2 * RADIX, h:Z_DIM]
        tslab_ref[j, :, h:Z_DIM] = res[0:RADIX, h:Z_DIM] + res[RADIX:2 * RADIX, 0:h]
    t_ref[...] = jnp.swapaxes(tslab_ref[...], 0, 1).astype(BF16)


def _dft_a(f_stage1, z):
    z3 = z.reshape(RADIX, RADIX, Z_DIM)
    block = (RADIX, DFT_COLS, Z_DIM)
    return pl.pallas_call(
        _dft_a_kernel,
        grid=(RADIX // DFT_COLS,),
        in_specs=[pl.BlockSpec((2 * RADIX, RADIX), lambda t: (0, 0)),
                  pl.BlockSpec(block, lambda t: (0, t, 0))],
        out_specs=pl.BlockSpec(block, lambda t: (0, t, 0)),
        out_shape=jax.ShapeDtypeStruct((RADIX, RADIX, Z_DIM), BF16),
        scratch_shapes=[pltpu.VMEM((DFT_COLS, RADIX, Z_DIM), F32), pltpu.VMEM((DFT_COLS, RADIX, Z_DIM), F32)],
        compiler_params=pltpu.CompilerParams(dimension_semantics=("arbitrary",),
                                             vmem_limit_bytes=VMEM_LIMIT),
        name="dft_a",
    )(f_stage1, z3)


def _mix_out_kernel(x_ref, yac_ref, t_ref, cphi_ref, sphi_ref, cth_ref, sth_ref, wout_ref,
                    w1_ref, w3_ref, w2_ref, gfin_ref, out_ref, yb_ref, ybuf, *, final_norm):
    chunk = pl.program_id(0)
    c0 = CONV_DIM
    h = FOURIER_DIM

    for sub in range(OUT_SUBTILES):
        local = slice(sub * D_ROWS, (sub + 1) * D_ROWS)
        cphi = cphi_ref[local, :]
        sphi = sphi_ref[local, :]
        for j in range(SUBLANES):
            c = chunk * SUBLANES + j
            cth = cth_ref[pl.ds(c, 1), :]
            sth = sth_ref[pl.ds(c, 1), :]
            mcos = (cphi * cth - sphi * sth).astype(BF16)
            msin = (sphi * cth + cphi * sth).astype(BF16)
            tj = t_ref[j]
            yb_ref[sub, :, j, :] = _dot(jnp.concatenate([mcos, msin], axis=1),
                                        jnp.concatenate([tj[:, 0:h], tj[:, h:Z_DIM]], axis=0))

        yac = yac_ref[local].reshape(TM_OUT, YAC_DIM)
        yb = yb_ref[sub].reshape(TM_OUT, h)
        ybuf[sub, :, 0:c0] = yac[:, 0:c0].astype(BF16)
        ybuf[sub, :, c0:c0 + h] = yb.astype(BF16)
        ybuf[sub, :, c0 + h:D_MODEL] = yac[:, c0:YAC_DIM].astype(BF16)

        x1 = x_ref[local].reshape(TM_OUT, D_MODEL) + _dot(ybuf[sub], wout_ref[...])
        xg = x1.astype(BF16)
        r = lax.rsqrt(jnp.mean(x1 * x1, axis=-1, keepdims=True) + EPS)
        acc = None
        start = 0
        for width in FF_CHUNKS:
            cols = slice(start, start + width)
            start += width
            gate = _dot(xg, w1_ref[:, cols]) * r
            up = _dot(xg, w3_ref[:, cols])
            hid = (gate * (1.0 / (1.0 + jnp.exp(-gate))) * up).astype(BF16)
            ffn = _dot(hid, w2_ref[cols, :])
            acc = ffn if acc is None else acc + ffn
        x2 = x1 + acc * r
        if final_norm:
            x2 = _unit_rms(x2) * gfin_ref[...]
        out_ref[local] = x2.reshape(D_ROWS, SUBLANES, D_MODEL)


def _mix_out(layer, x, yac, t, tabs, wout, w1, w3, w2, gfin, final_norm):
    x3 = x.reshape(RADIX, RADIX, D_MODEL)
    yac3 = yac.reshape(RADIX, RADIX, YAC_DIM)
    const = lambda s: (0, 0)
    lay = lambda s: (layer, 0, 0)
    tile = lambda s: (0, s, 0)
    once = dict(pipeline_mode=pl.Buffered(1))
    out = pl.pallas_call(
        functools.partial(_mix_out_kernel, final_norm=final_norm),
        grid=(RADIX // SUBLANES,),
        in_specs=[
            pl.BlockSpec((RADIX, SUBLANES, D_MODEL), tile),
            pl.BlockSpec((RADIX, SUBLANES, YAC_DIM), tile),
            pl.BlockSpec((SUBLANES, RADIX, Z_DIM), lambda s: (s, 0, 0)),
            pl.BlockSpec((RADIX, RADIX), const),
            pl.BlockSpec((RADIX, RADIX), const),
            pl.BlockSpec((RADIX, RADIX), const),
            pl.BlockSpec((RADIX, RADIX), const),
            pl.BlockSpec((None, D_MODEL, D_MODEL), lay, **once),
            pl.BlockSpec((D_MODEL, D_FF), const, **once),
            pl.BlockSpec((D_MODEL, D_FF), const, **once),
            pl.BlockSpec((D_FF, D_MODEL), const, **once),
            pl.BlockSpec((1, D_MODEL), const),
        ],
        out_specs=pl.BlockSpec((RADIX, SUBLANES, D_MODEL), tile),
        out_shape=jax.ShapeDtypeStruct((RADIX, RADIX, D_MODEL), F32),
        scratch_shapes=[pltpu.VMEM((OUT_SUBTILES, D_ROWS, SUBLANES, FOURIER_DIM), F32),
                        pltpu.VMEM((OUT_SUBTILES, TM_OUT, D_MODEL), BF16)],
        compiler_params=pltpu.CompilerParams(dimension_semantics=("arbitrary",),
                                             vmem_limit_bytes=VMEM_LIMIT),
        name="mix_out",
    )(x3, yac3, t, tabs["cphi"], tabs["sphi"], tabs["cth"], tabs["sth"], wout, w1, w3, w2, gfin)
    return out.reshape(SEQ, D_MODEL)


def kernel(x, g_mix, w_in, w_conv, w_fourier, w_pool, pool_scale, w_out, g_ffn, w1, w3, w2, g_final):
    tabs = {k: jnp.asarray(v) for k, v in _tables().items()}
    block_diag = jax.scipy.linalg.block_diag
    wf_bd = jnp.stack([block_diag(*[w_fourier[l, h] for h in range(FOURIER_HEADS)]) for l in range(DEPTH)])
    wp_bd = jnp.stack([block_diag(*[w_pool[l, g] for g in range(len(POOL_WINDOWS))]) for l in range(DEPTH)])
    gf_col = g_ffn.reshape(DEPTH, D_MODEL, 1)
    wout, win = _fold_weights(tabs["chan"], wf_bd, wp_bd, pool_scale, w_out, w_in, g_mix)
    f_stage1 = tabs["f_stage1"].astype(BF16)

    xs = x.reshape(SEQ, D_MODEL)
    for l in range(DEPTH):
        yac, z, w1b, w3b, w2b = _mix_in(l, xs, win, w_conv, gf_col, w1, w3, w2)
        t = _dft_a(f_stage1, z)
        xs = _mix_out(l, xs, yac, t, tabs, wout, w1b, w3b, w2b, g_final[None, :],
                      final_norm=(l == DEPTH - 1))
    return xs.reshape(1, SEQ, D_MODEL)
```

```python
import functools

import numpy as np
import jax
import jax.numpy as jnp
from jax import lax
from jax.experimental import pallas as pl
from jax.experimental.pallas import tpu as pltpu

D_MODEL = 1024
SEQ = 16384
DEPTH = 2
HEAD_DIM = 64
CONV_DIM = 384
FOURIER_HEADS = 4
FOURIER_DIM = 256
POOL_WINDOWS = (2, 4, 8, 16)
POOL_GROUP_DIM = 96
POOL_DIM = 384
IN_PROJ_DIM = 3 * CONV_DIM + FOURIER_DIM + POOL_DIM
D_FF = 2816
EPS = 1e-6

RADIX = 128
YAC_DIM = CONV_DIM + POOL_DIM
Z_DIM = 2 * FOURIER_DIM
LANES = 128
SUBLANES = 8
DFT_COLS = 16

TM_IN = 1024
HALO = 8
assert all(w // 2 == 2 ** k for k, w in enumerate(POOL_WINDOWS)) and POOL_WINDOWS[-1] // 2 <= HALO
OUT_SUBTILES = 2
D_ROWS = RADIX // OUT_SUBTILES
TM_OUT = D_ROWS * SUBLANES
MXU_DIM = 256
FF_CHUNKS = (1536, 1280)
assert sum(FF_CHUNKS) == D_FF and all(w % MXU_DIM == 0 for w in FF_CHUNKS)
VMEM_LIMIT = 56 * 1024 * 1024

F32 = jnp.float32
BF16 = jnp.bfloat16


def _dot(a, b):
    return jnp.dot(a, b, preferred_element_type=F32)


def _unit_rms(v):
    ms = jnp.mean(v * v, axis=-1, keepdims=True)
    return v * lax.rsqrt(ms + EPS)


def _tables():
    r = np.arange(RADIX, dtype=np.float64)
    ang128 = 2.0 * np.pi * np.outer(r, r) / RADIX
    angs = 2.0 * np.pi * np.outer(r, r) / SEQ
    f_stage1 = np.concatenate([np.cos(ang128), -np.sin(ang128)], axis=0)
    e = np.arange(HEAD_DIM, dtype=np.float64)
    angc = 2.0 * np.pi * np.outer(e, e) / HEAD_DIM
    norm = 1.0 / np.sqrt(float(SEQ) * HEAD_DIM)
    eye = np.eye(FOURIER_HEADS)
    cbd = np.kron(eye, np.cos(angc)) * norm
    sbd = np.kron(eye, -np.sin(angc)) * norm
    return dict(
        f_stage1=f_stage1.astype(np.float32),
        cphi=np.cos(ang128).astype(np.float32), sphi=np.sin(ang128).astype(np.float32),
        cth=np.cos(angs).astype(np.float32), sth=np.sin(angs).astype(np.float32),
        chan=np.stack([cbd, sbd]).astype(np.float32),
    )


WIN_CH = 0
WIN_P = 2 * CONV_DIM
WIN_B = 2 * CONV_DIM + POOL_DIM
WIN_F = 3 * CONV_DIM + POOL_DIM
WIN_DIM = WIN_F + Z_DIM
assert WIN_DIM % (2 * MXU_DIM) == 0
FOLD_PARTS = 4
FOLD_ROWS = D_MODEL // FOLD_PARTS


def _dot_3pass(a, b):
    a_hi = a.astype(BF16)
    b_hi = b.astype(BF16)
    a_lo = (a - a_hi.astype(F32)).astype(BF16)
    b_lo = (b - b_hi.astype(F32)).astype(BF16)
    return _dot(a_hi, b_hi) + (_dot(a_hi, b_lo) + _dot(a_lo, b_hi))


def _row_as_column(row):
    diagonal = (lax.broadcasted_iota(jnp.int32, (LANES, LANES), 0)
                == lax.broadcasted_iota(jnp.int32, (LANES, LANES), 1))
    blocks = [jnp.sum(jnp.where(diagonal, row[:, k:k + LANES], 0.0), axis=1, keepdims=True)
              for k in range(0, row.shape[1], LANES)]
    return jnp.concatenate(blocks, axis=0)


def _fold_kernel(chan_ref, wf_ref, wp_ref, ps_ref, wout_ref, win_ref, gmix_ref, gffn_ref,
                 woe_ref, wine_ref, gffn_col_ref, gain_ref, ab_ref, wfbd_ref, wpbd_ref):
    layer = pl.program_id(0)
    part = pl.program_id(1)

    @pl.when(part == 0)
    def _():
        gain_ref[...] = _row_as_column(gmix_ref[pl.ds(layer, 1), :])
        gffn_col_ref[...] = _row_as_column(gffn_ref[pl.ds(layer, 1), :])
        wfbd_ref[...] = jnp.zeros(wfbd_ref.shape, F32)
        for hd in range(FOURIER_HEADS):
            lo = hd * HEAD_DIM
            wfbd_ref[lo:lo + HEAD_DIM, lo:lo + HEAD_DIM] = wf_ref[hd]
        wpbd_ref[...] = jnp.zeros(wpbd_ref.shape, F32)
        for grp in range(len(POOL_WINDOWS)):
            lo = grp * POOL_GROUP_DIM
            wpbd_ref[lo:lo + POOL_GROUP_DIM, lo:lo + POOL_GROUP_DIM] = wp_ref[grp]
        wf = wfbd_ref[...]
        ab_ref[:, 0:FOURIER_DIM] = _dot_3pass(chan_ref[0], wf)
        ab_ref[:, FOURIER_DIM:Z_DIM] = _dot_3pass(chan_ref[1], wf)

    mixed = CONV_DIM + FOURIER_DIM
    woe_ref[0:mixed, :] = wout_ref[0:mixed, :].astype(BF16)
    woe_ref[mixed:D_MODEL, :] = _dot_3pass(wpbd_ref[...] * ps_ref[pl.ds(layer, 1), :],
                                           wout_ref[mixed:D_MODEL, :]).astype(BF16)
    c0 = CONV_DIM
    gain = gain_ref[pl.ds(pl.multiple_of(part * FOLD_ROWS, FOLD_ROWS), FOLD_ROWS), :]
    wine_ref[:, WIN_CH:WIN_P] = (win_ref[:, c0:3 * c0] * gain).astype(BF16)
    wine_ref[:, WIN_P:WIN_B] = (win_ref[:, 3 * c0 + FOURIER_DIM:IN_PROJ_DIM] * gain).astype(BF16)
    wine_ref[:, WIN_B:WIN_F] = (win_ref[:, 0:c0] * gain).astype(BF16)
    wfour = win_ref[:, 3 * c0:3 * c0 + FOURIER_DIM] * gain
    wine_ref[:, WIN_F:WIN_DIM] = _dot_3pass(wfour, ab_ref[...]).astype(BF16)


def _fold_weights(chan, w_fourier, w_pool, pool_scale, w_out, w_in, g_mix, g_ffn):
    lay = lambda l, q: (l, 0, 0)
    lay4 = lambda l, q: (l, 0, 0, 0)
    whole = lambda l, q: (0, 0)
    rows = lambda l, q: (l, q, 0)
    cols = lambda l, q: (l, 0, q)
    return pl.pallas_call(
        _fold_kernel,
        grid=(DEPTH, FOLD_PARTS),
        in_specs=[pl.BlockSpec((2, FOURIER_DIM, FOURIER_DIM), lambda l, q: (0, 0, 0)),
                  pl.BlockSpec((None, FOURIER_HEADS, HEAD_DIM, HEAD_DIM), lay4),
                  pl.BlockSpec((None, len(POOL_WINDOWS), POOL_GROUP_DIM, POOL_GROUP_DIM), lay4),
                  pl.BlockSpec((DEPTH, POOL_DIM), whole),
                  pl.BlockSpec((None, D_MODEL, FOLD_ROWS), cols),
                  pl.BlockSpec((None, FOLD_ROWS, IN_PROJ_DIM), rows),
                  pl.BlockSpec((DEPTH, D_MODEL), whole),
                  pl.BlockSpec((DEPTH, D_MODEL), whole)],
        out_specs=[pl.BlockSpec((None, D_MODEL, FOLD_ROWS), cols),
                   pl.BlockSpec((None, FOLD_ROWS, WIN_DIM), rows),
                   pl.BlockSpec((None, D_MODEL, 1), lay)],
        out_shape=[jax.ShapeDtypeStruct((DEPTH, D_MODEL, D_MODEL), BF16),
                   jax.ShapeDtypeStruct((DEPTH, D_MODEL, WIN_DIM), BF16),
                   jax.ShapeDtypeStruct((DEPTH, D_MODEL, 1), F32)],
        scratch_shapes=[pltpu.VMEM((D_MODEL, 1), F32), pltpu.VMEM((FOURIER_DIM, Z_DIM), F32),
                        pltpu.VMEM((FOURIER_DIM, FOURIER_DIM), F32), pltpu.VMEM((POOL_DIM, POOL_DIM), F32)],
        compiler_params=pltpu.CompilerParams(dimension_semantics=("arbitrary", "arbitrary")),
        name="fold_weights",
    )(chan, w_fourier, w_pool, pool_scale, w_out, w_in, g_mix, g_ffn)


def _window_sums(p, rows, max_half):
    sums = {1: p + pltpu.roll(p, 1, 0)}
    h = 1
    while h < max_half:
        sums[2 * h] = pltpu.roll(sums[h], h, 0) + pltpu.roll(sums[h], rows - h, 0)
        h *= 2
    return sums


def _mix_in_kernel(x_ref, xprev_ref, xnext_ref, win_ref, wconv_ref, gffn_ref, w1_ref, w3_ref, w2_ref,
                   yac_ref, z_ref, w1b_ref, w3b_ref, w2b_ref, xe_ref):
    gffn = gffn_ref[...]
    w1b_ref[...] = (w1_ref[...] * gffn).astype(BF16)
    w3b_ref[...] = (w3_ref[...] * gffn).astype(BF16)
    w2b_ref[...] = w2_ref[...].astype(BF16)

    i = pl.program_id(0)
    n = pl.num_programs(0)
    rows = TM_IN + 2 * HALO
    prev = jnp.where(i > 0, _unit_rms(xprev_ref[...]), 0.0)
    nxt = jnp.where(i < n - 1, _unit_rms(xnext_ref[...]), 0.0)
    xe_ref[0:TM_IN, :] = _unit_rms(x_ref[...]).astype(BF16)
    xe_ref[TM_IN:rows, :] = jnp.concatenate([nxt, prev], axis=0).astype(BF16)

    z = _dot(xe_ref[...], win_ref[...])
    c0 = CONV_DIM
    tile = slice(0, TM_IN)
    zp = z[:, WIN_P:WIN_B]
    z_ref[...] = z[tile, WIN_F:WIN_DIM].astype(BF16)

    u = z[:, WIN_CH:WIN_CH + c0] * z[:, WIN_CH + c0:WIN_P]
    conv = (wconv_ref[0:1, :] * pltpu.roll(u, 1, 0) + wconv_ref[1:2, :] * u
            + wconv_ref[2:3, :] * pltpu.roll(u, rows - 1, 0))
    yac_ref[:, 0:c0] = z[tile, WIN_B:WIN_F] * conv[tile]

    lane_tiles = []
    for lo in range(0, POOL_DIM, LANES):
        p = zp[:, lo:lo + LANES]
        groups = range(lo // POOL_GROUP_DIM, (lo + LANES - 1) // POOL_GROUP_DIM + 1)
        sums = _window_sums(p, rows, POOL_WINDOWS[groups[-1]] // 2)
        lane = lo + lax.broadcasted_iota(jnp.int32, (1, LANES), 1)
        wsum = sums[POOL_WINDOWS[groups[-1]] // 2][tile]
        half = jnp.full((1, LANES), POOL_WINDOWS[groups[-1]] // 2, jnp.int32)
        for grp in reversed(groups[:-1]):
            inside = lane < (grp + 1) * POOL_GROUP_DIM
            wsum = jnp.where(inside, sums[POOL_WINDOWS[grp] // 2][tile], wsum)
            half = jnp.where(inside, POOL_WINDOWS[grp] // 2, half)
        cols = slice(c0 + lo, c0 + lo + LANES)
        yac_ref[:, cols] = wsum * (1.0 / (2 * half).astype(F32)) - p[tile]
        lane_tiles.append((cols, wsum, half, p))

    def clipped_window_rows(local, first_t):
        t = first_t + lax.broadcasted_iota(jnp.int32, (HALO, LANES), 0)
        for cols, wsum, half, p in lane_tiles:
            cnt = (jnp.minimum(t + half, SEQ) - jnp.maximum(t - half, 0)).astype(F32)
            yac_ref[local, cols] = wsum[local] / cnt - p[local]

    @pl.when(i == 0)
    def _():
        clipped_window_rows(slice(0, HALO), 0)

    @pl.when(i == n - 1)
    def _():
        clipped_window_rows(slice(TM_IN - HALO, TM_IN), SEQ - HALO)


def _mix_in(layer, x, win, wconv, gffn_col, w1, w3, w2):
    n = SEQ // TM_IN
    hb = TM_IN // HALO
    lay = lambda i: (layer, 0, 0)
    w_rows = lambda i: (layer, i, 0)
    cast_rows = lambda i: (i, 0)
    return pl.pallas_call(
        _mix_in_kernel,
        grid=(n,),
        in_specs=[
            pl.BlockSpec((TM_IN, D_MODEL), lambda i: (i, 0)),
            pl.BlockSpec((HALO, D_MODEL), lambda i: (jnp.maximum(i * hb - 1, 0), 0)),
            pl.BlockSpec((HALO, D_MODEL), lambda i: (jnp.minimum((i + 1) * hb, SEQ // HALO - 1), 0)),
            pl.BlockSpec((None, D_MODEL, WIN_DIM), lay, pipeline_mode=pl.Buffered(1)),
            pl.BlockSpec((None, 3, CONV_DIM), lay),
            pl.BlockSpec((None, D_MODEL // n, 1), w_rows),
            pl.BlockSpec((None, D_MODEL // n, D_FF), w_rows),
            pl.BlockSpec((None, D_MODEL // n, D_FF), w_rows),
            pl.BlockSpec((None, D_FF // n, D_MODEL), w_rows),
        ],
        out_specs=[pl.BlockSpec((TM_IN, YAC_DIM), lambda i: (i, 0)),
                   pl.BlockSpec((TM_IN, Z_DIM), lambda i: (i, 0)),
                   pl.BlockSpec((D_MODEL // n, D_FF), cast_rows),
                   pl.BlockSpec((D_MODEL // n, D_FF), cast_rows),
                   pl.BlockSpec((D_FF // n, D_MODEL), cast_rows)],
        out_shape=[jax.ShapeDtypeStruct((SEQ, YAC_DIM), F32),
                   jax.ShapeDtypeStruct((SEQ, Z_DIM), BF16),
                   jax.ShapeDtypeStruct((D_MODEL, D_FF), BF16),
                   jax.ShapeDtypeStruct((D_MODEL, D_FF), BF16),
                   jax.ShapeDtypeStruct((D_FF, D_MODEL), BF16)],
        scratch_shapes=[pltpu.VMEM((TM_IN + 2 * HALO, D_MODEL), BF16)],
        compiler_params=pltpu.CompilerParams(dimension_semantics=("arbitrary",),
                                             vmem_limit_bytes=VMEM_LIMIT),
        name="mix_in",
    )(x, x, x, win, wconv, gffn_col, w1, w3, w2)


def _dft_a_kernel(f_ref, z_ref, t_ref, zslab_ref, tslab_ref):
    h = FOURIER_DIM
    f = f_ref[...]
    zslab_ref[...] = jnp.swapaxes(z_ref[...].astype(F32), 0, 1)
    for j in range(DFT_COLS):
        res = _dot(f, zslab_ref[j].astype(BF16))
        tslab_ref[j, :, 0:h] = res[0:RADIX, 0:h] - res[RADIX:2 * RADIX, h:Z_DIM]
        tslab_ref[j, :, h:Z_DIM] = res[0:RADIX, h:Z_DIM] + res[RADIX:2 * RADIX, 0:h]
    t_ref[...] = jnp.swapaxes(tslab_ref[...], 0, 1).astype(BF16)


def _dft_a(f_stage1, z):
    z3 = z.reshape(RADIX, RADIX, Z_DIM)
    block = (RADIX, DFT_COLS, Z_DIM)
    return pl.pallas_call(
        _dft_a_kernel,
        grid=(RADIX // DFT_COLS,),
        in_specs=[pl.BlockSpec((2 * RADIX, RADIX), lambda t: (0, 0)),
                  pl.BlockSpec(block, lambda t: (0, t, 0))],
        out_specs=pl.BlockSpec(block, lambda t: (0, t, 0)),
        out_shape=jax.ShapeDtypeStruct((RADIX, RADIX, Z_DIM), BF16),
        scratch_shapes=[pltpu.VMEM((DFT_COLS, RADIX, Z_DIM), F32), pltpu.VMEM((DFT_COLS, RADIX, Z_DIM), F32)],
        compiler_params=pltpu.CompilerParams(dimension_semantics=("arbitrary",)),
        name="dft_a",
    )(f_stage1, z3)


def _mix_out_kernel(x_ref, yac_ref, t_ref, cphi_ref, sphi_ref, cth_ref, sth_ref, wout_ref,
                    w1_ref, w3_ref, w2_ref, gfin_ref, out_ref, yb_ref, ybuf, *, final_norm):
    chunk = pl.program_id(0)
    c0 = CONV_DIM
    h = FOURIER_DIM

    for sub in range(OUT_SUBTILES):
        local = slice(sub * D_ROWS, (sub + 1) * D_ROWS)
        cphi = cphi_ref[local, :]
        sphi = sphi_ref[local, :]
        for j in range(SUBLANES):
            c = chunk * SUBLANES + j
            cth = cth_ref[pl.ds(c, 1), :]
            sth = sth_ref[pl.ds(c, 1), :]
            mcos = (cphi * cth - sphi * sth).astype(BF16)
            msin = (sphi * cth + cphi * sth).astype(BF16)
            tj = t_ref[j]
            yb_ref[sub, :, j, :] = _dot(jnp.concatenate([mcos, msin], axis=1),
                                        jnp.concatenate([tj[:, 0:h], tj[:, h:Z_DIM]], axis=0))

        yac = yac_ref[local].reshape(TM_OUT, YAC_DIM)
        yb = yb_ref[sub].reshape(TM_OUT, h)
        ybuf[sub, :, 0:c0] = yac[:, 0:c0].astype(BF16)
        ybuf[sub, :, c0:c0 + h] = yb.astype(BF16)
        ybuf[sub, :, c0 + h:D_MODEL] = yac[:, c0:YAC_DIM].astype(BF16)

        x1 = x_ref[local].reshape(TM_OUT, D_MODEL) + _dot(ybuf[sub], wout_ref[...])
        xg = x1.astype(BF16)
        r = lax.rsqrt(jnp.mean(x1 * x1, axis=-1, keepdims=True) + EPS)
        acc = None
        start = 0
        for width in FF_CHUNKS:
            cols = slice(start, start + width)
            start += width
            gate = _dot(xg, w1_ref[:, cols]) * r
            up = _dot(xg, w3_ref[:, cols])
            hid = (gate * (1.0 / (1.0 + jnp.exp(-gate))) * up).astype(BF16)
            ffn = _dot(hid, w2_ref[cols, :])
            acc = ffn if acc is None else acc + ffn
        x2 = x1 + acc * r
        if final_norm:
            x2 = _unit_rms(x2) * gfin_ref[...]
        out_ref[local] = x2.reshape(D_ROWS, SUBLANES, D_MODEL)


def _mix_out(layer, x, yac, t, tabs, wout, w1, w3, w2, gfin, final_norm):
    x3 = x.reshape(RADIX, RADIX, D_MODEL)
    yac3 = yac.reshape(RADIX, RADIX, YAC_DIM)
    const = lambda s: (0, 0)
    lay = lambda s: (layer, 0, 0)
    tile = lambda s: (0, s, 0)
    once = dict(pipeline_mode=pl.Buffered(1))
    out = pl.pallas_call(
        functools.partial(_mix_out_kernel, final_norm=final_norm),
        grid=(RADIX // SUBLANES,),
        in_specs=[
            pl.BlockSpec((RADIX, SUBLANES, D_MODEL), tile),
            pl.BlockSpec((RADIX, SUBLANES, YAC_DIM), tile),
            pl.BlockSpec((SUBLANES, RADIX, Z_DIM), lambda s: (s, 0, 0)),
            pl.BlockSpec((RADIX, RADIX), const),
            pl.BlockSpec((RADIX, RADIX), const),
            pl.BlockSpec((RADIX, RADIX), const),
            pl.BlockSpec((RADIX, RADIX), const),
            pl.BlockSpec((None, D_MODEL, D_MODEL), lay, **once),
            pl.BlockSpec((D_MODEL, D_FF), const, **once),
            pl.BlockSpec((D_MODEL, D_FF), const, **once),
            pl.BlockSpec((D_FF, D_MODEL), const, **once),
            pl.BlockSpec((1, D_MODEL), const),
        ],
        out_specs=pl.BlockSpec((RADIX, SUBLANES, D_MODEL), tile),
        out_shape=jax.ShapeDtypeStruct((RADIX, RADIX, D_MODEL), F32),
        scratch_shapes=[pltpu.VMEM((OUT_SUBTILES, D_ROWS, SUBLANES, FOURIER_DIM), F32),
                        pltpu.VMEM((OUT_SUBTILES, TM_OUT, D_MODEL), BF16)],
        compiler_params=pltpu.CompilerParams(dimension_semantics=("arbitrary",),
                                             vmem_limit_bytes=VMEM_LIMIT),
        name="mix_out",
    )(x3, yac3, t, tabs["cphi"], tabs["sphi"], tabs["cth"], tabs["sth"], wout, w1, w3, w2, gfin)
    return out.reshape(SEQ, D_MODEL)


def kernel(x, g_mix, w_in, w_conv, w_fourier, w_pool, pool_scale, w_out, g_ffn, w1, w3, w2, g_final):
    tabs = {k: jnp.asarray(v) for k, v in _tables().items()}
    wout, win, gf_col = _fold_weights(tabs["chan"], w_fourier, w_pool, pool_scale, w_out, w_in, g_mix, g_ffn)
    f_stage1 = tabs["f_stage1"].astype(BF16)

    xs = x.reshape(SEQ, D_MODEL)
    for l in range(DEPTH):
        yac, z, w1b, w3b, w2b = _mix_in(l, xs, win, w_conv, gf_col, w1, w3, w2)
        t = _dft_a(f_stage1, z)
        xs = _mix_out(l, xs, yac, t, tabs, wout, w1b, w3b, w2b, g_final[None, :],
                      final_norm=(l == DEPTH - 1))
    return xs.reshape(1, SEQ, D_MODEL)
```

```python
import functools

import numpy as np
import jax
import jax.numpy as jnp
from jax import lax
from jax.experimental import pallas as pl
from jax.experimental.pallas import tpu as pltpu

D_MODEL = 1024
SEQ = 16384
DEPTH = 2
HEAD_DIM = 64
CONV_DIM = 384
FOURIER_HEADS = 4
FOURIER_DIM = 256
POOL_WINDOWS = (2, 4, 8, 16)
POOL_GROUP_DIM = 96
POOL_DIM = 384
IN_PROJ_DIM = 3 * CONV_DIM + FOURIER_DIM + POOL_DIM
D_FF = 2816
EPS = 1e-6

RADIX = 128
YAC_DIM = CONV_DIM + POOL_DIM
Z_DIM = 2 * FOURIER_DIM
LANES = 128
SUBLANES = 8
DFT_COLS = 16

TM_IN = 1024
X_SLOTS = 3
HALO = 8
assert all(w // 2 == 2 ** k for k, w in enumerate(POOL_WINDOWS)) and POOL_WINDOWS[-1] // 2 <= HALO
OUT_SUBTILES = 2
D_ROWS = RADIX // OUT_SUBTILES
TM_OUT = D_ROWS * SUBLANES
MXU_DIM = 256
FF_CHUNKS = (1536, 1280)
assert sum(FF_CHUNKS) == D_FF and all(w % MXU_DIM == 0 for w in FF_CHUNKS)
VMEM_LIMIT = 56 * 1024 * 1024

F32 = jnp.float32
BF16 = jnp.bfloat16


def _dot(a, b):
    return jnp.dot(a, b, preferred_element_type=F32)


def _unit_rms(v):
    ms = jnp.mean(v * v, axis=-1, keepdims=True)
    return v * lax.rsqrt(ms + EPS)


def _tables():
    r = np.arange(RADIX, dtype=np.float64)
    ang128 = 2.0 * np.pi * np.outer(r, r) / RADIX
    angs = 2.0 * np.pi * np.outer(r, r) / SEQ
    f_stage1 = np.concatenate([np.cos(ang128), -np.sin(ang128)], axis=0)
    e = np.arange(HEAD_DIM, dtype=np.float64)
    angc = 2.0 * np.pi * np.outer(e, e) / HEAD_DIM
    norm = 1.0 / np.sqrt(float(SEQ) * HEAD_DIM)
    eye = np.eye(FOURIER_HEADS)
    cbd = np.kron(eye, np.cos(angc)) * norm
    sbd = np.kron(eye, -np.sin(angc)) * norm
    return dict(
        f_stage1=f_stage1.astype(np.float32),
        cphi=np.cos(ang128).astype(np.float32), sphi=np.sin(ang128).astype(np.float32),
        cth=np.cos(angs).astype(np.float32), sth=np.sin(angs).astype(np.float32),
        chan=np.stack([cbd, sbd]).astype(np.float32),
    )


WIN_CH = 0
WIN_P = 2 * CONV_DIM
WIN_B = 2 * CONV_DIM + POOL_DIM
WIN_F = 3 * CONV_DIM + POOL_DIM
WIN_DIM = WIN_F + Z_DIM
assert WIN_DIM % (2 * MXU_DIM) == 0
FOLD_PARTS = 4
FOLD_ROWS = D_MODEL // FOLD_PARTS


def _dot_3pass(a, b):
    a_hi = a.astype(BF16)
    b_hi = b.astype(BF16)
    a_lo = (a - a_hi.astype(F32)).astype(BF16)
    b_lo = (b - b_hi.astype(F32)).astype(BF16)
    return _dot(a_hi, b_hi) + (_dot(a_hi, b_lo) + _dot(a_lo, b_hi))


def _row_as_column(row):
    diagonal = (lax.broadcasted_iota(jnp.int32, (LANES, LANES), 0)
                == lax.broadcasted_iota(jnp.int32, (LANES, LANES), 1))
    blocks = [jnp.sum(jnp.where(diagonal, row[:, k:k + LANES], 0.0), axis=1, keepdims=True)
              for k in range(0, row.shape[1], LANES)]
    return jnp.concatenate(blocks, axis=0)


def _fold_kernel(chan_ref, wf_ref, wp_ref, ps_ref, wout_ref, win_ref, gmix_ref, gffn_ref,
                 woe_ref, wine_ref, gffn_col_ref, gain_ref, ab_ref, wfbd_ref, wpbd_ref):
    layer = pl.program_id(0)
    part = pl.program_id(1)

    @pl.when(part == 0)
    def _():
        gain_ref[...] = _row_as_column(gmix_ref[pl.ds(layer, 1), :])
        gffn_col_ref[...] = _row_as_column(gffn_ref[pl.ds(layer, 1), :])
        wfbd_ref[...] = jnp.zeros(wfbd_ref.shape, F32)
        for hd in range(FOURIER_HEADS):
            lo = hd * HEAD_DIM
            wfbd_ref[lo:lo + HEAD_DIM, lo:lo + HEAD_DIM] = wf_ref[hd]
        wpbd_ref[...] = jnp.zeros(wpbd_ref.shape, F32)
        for grp in range(len(POOL_WINDOWS)):
            lo = grp * POOL_GROUP_DIM
            wpbd_ref[lo:lo + POOL_GROUP_DIM, lo:lo + POOL_GROUP_DIM] = wp_ref[grp]
        wf = wfbd_ref[...]
        ab_ref[:, 0:FOURIER_DIM] = _dot_3pass(chan_ref[0], wf)
        ab_ref[:, FOURIER_DIM:Z_DIM] = _dot_3pass(chan_ref[1], wf)

    mixed = CONV_DIM + FOURIER_DIM
    woe_ref[0:mixed, :] = wout_ref[0:mixed, :].astype(BF16)
    woe_ref[mixed:D_MODEL, :] = _dot_3pass(wpbd_ref[...] * ps_ref[pl.ds(layer, 1), :],
                                           wout_ref[mixed:D_MODEL, :]).astype(BF16)
    c0 = CONV_DIM
    gain = gain_ref[pl.ds(pl.multiple_of(part * FOLD_ROWS, FOLD_ROWS), FOLD_ROWS), :]
    wine_ref[:, WIN_CH:WIN_P] = (win_ref[:, c0:3 * c0] * gain).astype(BF16)
    wine_ref[:, WIN_P:WIN_B] = (win_ref[:, 3 * c0 + FOURIER_DIM:IN_PROJ_DIM] * gain).astype(BF16)
    wine_ref[:, WIN_B:WIN_F] = (win_ref[:, 0:c0] * gain).astype(BF16)
    wfour = win_ref[:, 3 * c0:3 * c0 + FOURIER_DIM] * gain
    wine_ref[:, WIN_F:WIN_DIM] = _dot_3pass(wfour, ab_ref[...]).astype(BF16)


def _fold_weights(chan, w_fourier, w_pool, pool_scale, w_out, w_in, g_mix, g_ffn):
    lay = lambda l, q: (l, 0, 0)
    lay4 = lambda l, q: (l, 0, 0, 0)
    whole = lambda l, q: (0, 0)
    rows = lambda l, q: (l, q, 0)
    cols = lambda l, q: (l, 0, q)
    return pl.pallas_call(
        _fold_kernel,
        grid=(DEPTH, FOLD_PARTS),
        in_specs=[pl.BlockSpec((2, FOURIER_DIM, FOURIER_DIM), lambda l, q: (0, 0, 0)),
                  pl.BlockSpec((None, FOURIER_HEADS, HEAD_DIM, HEAD_DIM), lay4),
                  pl.BlockSpec((None, len(POOL_WINDOWS), POOL_GROUP_DIM, POOL_GROUP_DIM), lay4),
                  pl.BlockSpec((DEPTH, POOL_DIM), whole),
                  pl.BlockSpec((None, D_MODEL, FOLD_ROWS), cols),
                  pl.BlockSpec((None, FOLD_ROWS, IN_PROJ_DIM), rows),
                  pl.BlockSpec((DEPTH, D_MODEL), whole),
                  pl.BlockSpec((DEPTH, D_MODEL), whole)],
        out_specs=[pl.BlockSpec((None, D_MODEL, FOLD_ROWS), cols),
                   pl.BlockSpec((None, FOLD_ROWS, WIN_DIM), rows),
                   pl.BlockSpec((None, D_MODEL, 1), lay)],
        out_shape=[jax.ShapeDtypeStruct((DEPTH, D_MODEL, D_MODEL), BF16),
                   jax.ShapeDtypeStruct((DEPTH, D_MODEL, WIN_DIM), BF16),
                   jax.ShapeDtypeStruct((DEPTH, D_MODEL, 1), F32)],
        scratch_shapes=[pltpu.VMEM((D_MODEL, 1), F32), pltpu.VMEM((FOURIER_DIM, Z_DIM), F32),
                        pltpu.VMEM((FOURIER_DIM, FOURIER_DIM), F32), pltpu.VMEM((POOL_DIM, POOL_DIM), F32)],
        compiler_params=pltpu.CompilerParams(dimension_semantics=("arbitrary", "arbitrary")),
        name="fold_weights",
    )(chan, w_fourier, w_pool, pool_scale, w_out, w_in, g_mix, g_ffn)


def _window_sums(p, rows, max_half):
    sums = {1: p + pltpu.roll(p, 1, 0)}
    h = 1
    while h < max_half:
        sums[2 * h] = pltpu.roll(sums[h], h, 0) + pltpu.roll(sums[h], rows - h, 0)
        h *= 2
    return sums


def _mix_in_kernel(x_hbm, xprev_ref, xnext_ref, win_ref, wconv_ref, gffn_ref, w1_ref, w3_ref, w2_ref,
                   yac_ref, z_ref, w1b_ref, w3b_ref, w2b_ref, xe_ref, x_ring, x_sem):
    i = pl.program_id(0)
    n = pl.num_programs(0)

    def tile_copy(t):
        slot = lax.rem(t, X_SLOTS)
        return pltpu.make_async_copy(x_hbm.at[pl.ds(pl.multiple_of(t * TM_IN, TM_IN), TM_IN), :],
                                     x_ring.at[slot], x_sem.at[slot])

    @pl.when(i == 0)
    def _():
        tile_copy(0).start()
        tile_copy(1).start()

    @pl.when(i + 2 < n)
    def _():
        tile_copy(i + 2).start()

    tile_copy(i).wait()
    x_ref = x_ring.at[lax.rem(i, X_SLOTS)]

    gffn = gffn_ref[...]
    w1b_ref[...] = (w1_ref[...] * gffn).astype(BF16)
    w3b_ref[...] = (w3_ref[...] * gffn).astype(BF16)
    w2b_ref[...] = w2_ref[...].astype(BF16)

    rows = TM_IN + 2 * HALO
    prev = jnp.where(i > 0, _unit_rms(xprev_ref[...]), 0.0)
    nxt = jnp.where(i < n - 1, _unit_rms(xnext_ref[...]), 0.0)
    xe_ref[0:TM_IN, :] = _unit_rms(x_ref[...]).astype(BF16)
    xe_ref[TM_IN:rows, :] = jnp.concatenate([nxt, prev], axis=0).astype(BF16)

    z = _dot(xe_ref[...], win_ref[...])
    c0 = CONV_DIM
    tile = slice(0, TM_IN)
    zp = z[:, WIN_P:WIN_B]
    z_ref[...] = z[tile, WIN_F:WIN_DIM].astype(BF16)

    u = z[:, WIN_CH:WIN_CH + c0] * z[:, WIN_CH + c0:WIN_P]
    conv = (wconv_ref[0:1, :] * pltpu.roll(u, 1, 0) + wconv_ref[1:2, :] * u
            + wconv_ref[2:3, :] * pltpu.roll(u, rows - 1, 0))
    yac_ref[:, 0:c0] = z[tile, WIN_B:WIN_F] * conv[tile]

    lane_tiles = []
    for lo in range(0, POOL_DIM, LANES):
        p = zp[:, lo:lo + LANES]
        groups = range(lo // POOL_GROUP_DIM, (lo + LANES - 1) // POOL_GROUP_DIM + 1)
        sums = _window_sums(p, rows, POOL_WINDOWS[groups[-1]] // 2)
        lane = lo + lax.broadcasted_iota(jnp.int32, (1, LANES), 1)
        wsum = sums[POOL_WINDOWS[groups[-1]] // 2][tile]
        half = jnp.full((1, LANES), POOL_WINDOWS[groups[-1]] // 2, jnp.int32)
        for grp in reversed(groups[:-1]):
            inside = lane < (grp + 1) * POOL_GROUP_DIM
            wsum = jnp.where(inside, sums[POOL_WINDOWS[grp] // 2][tile], wsum)
            half = jnp.where(inside, POOL_WINDOWS[grp] // 2, half)
        cols = slice(c0 + lo, c0 + lo + LANES)
        yac_ref[:, cols] = wsum * (1.0 / (2 * half).astype(F32)) - p[tile]
        lane_tiles.append((cols, wsum, half, p))

    def clipped_window_rows(local, first_t):
        t = first_t + lax.broadcasted_iota(jnp.int32, (HALO, LANES), 0)
        for cols, wsum, half, p in lane_tiles:
            cnt = (jnp.minimum(t + half, SEQ) - jnp.maximum(t - half, 0)).astype(F32)
            yac_ref[local, cols] = wsum[local] / cnt - p[local]

    @pl.when(i == 0)
    def _():
        clipped_window_rows(slice(0, HALO), 0)

    @pl.when(i == n - 1)
    def _():
        clipped_window_rows(slice(TM_IN - HALO, TM_IN), SEQ - HALO)


def _mix_in(layer, x, win, wconv, gffn_col, w1, w3, w2):
    n = SEQ // TM_IN
    hb = TM_IN // HALO
    lay = lambda i: (layer, 0, 0)
    w_rows = lambda i: (layer, i, 0)
    cast_rows = lambda i: (i, 0)
    return pl.pallas_call(
        _mix_in_kernel,
        grid=(n,),
        in_specs=[
            pl.BlockSpec(memory_space=pl.ANY),
            pl.BlockSpec((HALO, D_MODEL), lambda i: (jnp.maximum(i * hb - 1, 0), 0)),
            pl.BlockSpec((HALO, D_MODEL), lambda i: (jnp.minimum((i + 1) * hb, SEQ // HALO - 1), 0)),
            pl.BlockSpec((None, D_MODEL, WIN_DIM), lay, pipeline_mode=pl.Buffered(1)),
            pl.BlockSpec((None, 3, CONV_DIM), lay),
            pl.BlockSpec((None, D_MODEL // n, 1), w_rows),
            pl.BlockSpec((None, D_MODEL // n, D_FF), w_rows),
            pl.BlockSpec((None, D_MODEL // n, D_FF), w_rows),
            pl.BlockSpec((None, D_FF // n, D_MODEL), w_rows),
        ],
        out_specs=[pl.BlockSpec((TM_IN, YAC_DIM), lambda i: (i, 0)),
                   pl.BlockSpec((TM_IN, Z_DIM), lambda i: (i, 0)),
                   pl.BlockSpec((D_MODEL // n, D_FF), cast_rows),
                   pl.BlockSpec((D_MODEL // n, D_FF), cast_rows),
                   pl.BlockSpec((D_FF // n, D_MODEL), cast_rows)],
        out_shape=[jax.ShapeDtypeStruct((SEQ, YAC_DIM), F32),
                   jax.ShapeDtypeStruct((SEQ, Z_DIM), BF16),
                   jax.ShapeDtypeStruct((D_MODEL, D_FF), BF16),
                   jax.ShapeDtypeStruct((D_MODEL, D_FF), BF16),
                   jax.ShapeDtypeStruct((D_FF, D_MODEL), BF16)],
        scratch_shapes=[pltpu.VMEM((TM_IN + 2 * HALO, D_MODEL), BF16),
                        pltpu.VMEM((X_SLOTS, TM_IN, D_MODEL), F32),
                        pltpu.SemaphoreType.DMA((X_SLOTS,))],
        compiler_params=pltpu.CompilerParams(dimension_semantics=("arbitrary",),
                                             vmem_limit_bytes=VMEM_LIMIT),
        name="mix_in",
    )(x, x, x, win, wconv, gffn_col, w1, w3, w2)


def _dft_a_kernel(f_ref, z_ref, t_ref, zslab_ref, tslab_ref):
    h = FOURIER_DIM
    f = f_ref[...]
    zslab_ref[...] = jnp.swapaxes(z_ref[...].astype(F32), 0, 1)
    for j in range(DFT_COLS):
        res = _dot(f, zslab_ref[j].astype(BF16))
        tslab_ref[j, :, 0:h] = res[0:RADIX, 0:h] - res[RADIX:2 * RADIX, h:Z_DIM]
        tslab_ref[j, :, h:Z_DIM] = res[0:RADIX, h:Z_DIM] + res[RADIX:2 * RADIX, 0:h]
    t_ref[...] = jnp.swapaxes(tslab_ref[...], 0, 1).astype(BF16)


def _dft_a(f_stage1, z):
    z3 = z.reshape(RADIX, RADIX, Z_DIM)
    block = (RADIX, DFT_COLS, Z_DIM)
    return pl.pallas_call(
        _dft_a_kernel,
        grid=(RADIX // DFT_COLS,),
        in_specs=[pl.BlockSpec((2 * RADIX, RADIX), lambda t: (0, 0)),
                  pl.BlockSpec(block, lambda t: (0, t, 0))],
        out_specs=pl.BlockSpec(block, lambda t: (0, t, 0)),
        out_shape=jax.ShapeDtypeStruct((RADIX, RADIX, Z_DIM), BF16),
        scratch_shapes=[pltpu.VMEM((DFT_COLS, RADIX, Z_DIM), F32), pltpu.VMEM((DFT_COLS, RADIX, Z_DIM), F32)],
        compiler_params=pltpu.CompilerParams(dimension_semantics=("arbitrary",)),
        name="dft_a",
    )(f_stage1, z3)


def _mix_out_kernel(x_ref, yac_ref, t_ref, cphi_ref, sphi_ref, cth_ref, sth_ref, wout_ref,
                    w1_ref, w3_ref, w2_ref, gfin_ref, out_ref, yb_ref, ybuf, *, final_norm):
    chunk = pl.program_id(0)
    c0 = CONV_DIM
    h = FOURIER_DIM

    for sub in range(OUT_SUBTILES):
        local = slice(sub * D_ROWS, (sub + 1) * D_ROWS)
        cphi = cphi_ref[local, :]
        sphi = sphi_ref[local, :]
        for j in range(SUBLANES):
            c = chunk * SUBLANES + j
            cth = cth_ref[pl.ds(c, 1), :]
            sth = sth_ref[pl.ds(c, 1), :]
            mcos = (cphi * cth - sphi * sth).astype(BF16)
            msin = (sphi * cth + cphi * sth).astype(BF16)
            tj = t_ref[j]
            yb_ref[sub, :, j, :] = _dot(jnp.concatenate([mcos, msin], axis=1),
                                        jnp.concatenate([tj[:, 0:h], tj[:, h:Z_DIM]], axis=0))

        yac = yac_ref[local].reshape(TM_OUT, YAC_DIM)
        yb = yb_ref[sub].reshape(TM_OUT, h)
        ybuf[sub, :, 0:c0] = yac[:, 0:c0].astype(BF16)
        ybuf[sub, :, c0:c0 + h] = yb.astype(BF16)
        ybuf[sub, :, c0 + h:D_MODEL] = yac[:, c0:YAC_DIM].astype(BF16)

        x1 = x_ref[local].reshape(TM_OUT, D_MODEL) + _dot(ybuf[sub], wout_ref[...])
        xg = x1.astype(BF16)
        r = lax.rsqrt(jnp.mean(x1 * x1, axis=-1, keepdims=True) + EPS)
        acc = None
        start = 0
        for width in FF_CHUNKS:
            cols = slice(start, start + width)
            start += width
            gate = _dot(xg, w1_ref[:, cols]) * r
            up = _dot(xg, w3_ref[:, cols])
            hid = (gate * (1.0 / (1.0 + jnp.exp(-gate))) * up).astype(BF16)
            ffn = _dot(hid, w2_ref[cols, :])
            acc = ffn if acc is None else acc + ffn
        x2 = x1 + acc * r
        if final_norm:
            x2 = _unit_rms(x2) * gfin_ref[...]
        out_ref[local] = x2.reshape(D_ROWS, SUBLANES, D_MODEL)


def _mix_out(layer, x, yac, t, tabs, wout, w1, w3, w2, gfin, final_norm):
    x3 = x.reshape(RADIX, RADIX, D_MODEL)
    yac3 = yac.reshape(RADIX, RADIX, YAC_DIM)
    const = lambda s: (0, 0)
    lay = lambda s: (layer, 0, 0)
    tile = lambda s: (0, s, 0)
    once = dict(pipeline_mode=pl.Buffered(1))
    out = pl.pallas_call(
        functools.partial(_mix_out_kernel, final_norm=final_norm),
        grid=(RADIX // SUBLANES,),
        in_specs=[
            pl.BlockSpec((RADIX, SUBLANES, D_MODEL), tile),
            pl.BlockSpec((RADIX, SUBLANES, YAC_DIM), tile),
            pl.BlockSpec((SUBLANES, RADIX, Z_DIM), lambda s: (s, 0, 0)),
            pl.BlockSpec((RADIX, RADIX), const),
            pl.BlockSpec((RADIX, RADIX), const),
            pl.BlockSpec((RADIX, RADIX), const),
            pl.BlockSpec((RADIX, RADIX), const),
            pl.BlockSpec((None, D_MODEL, D_MODEL), lay, **once),
            pl.BlockSpec((D_MODEL, D_FF), const, **once),
            pl.BlockSpec((D_MODEL, D_FF), const, **once),
            pl.BlockSpec((D_FF, D_MODEL), const, **once),
            pl.BlockSpec((1, D_MODEL), const),
        ],
        out_specs=pl.BlockSpec((RADIX, SUBLANES, D_MODEL), tile),
        out_shape=jax.ShapeDtypeStruct((RADIX, RADIX, D_MODEL), F32),
        scratch_shapes=[pltpu.VMEM((OUT_SUBTILES, D_ROWS, SUBLANES, FOURIER_DIM), F32),
                        pltpu.VMEM((OUT_SUBTILES, TM_OUT, D_MODEL), BF16)],
        compiler_params=pltpu.CompilerParams(dimension_semantics=("arbitrary",),
                                             vmem_limit_bytes=VMEM_LIMIT),
        name="mix_out",
    )(x3, yac3, t, tabs["cphi"], tabs["sphi"], tabs["cth"], tabs["sth"], wout, w1, w3, w2, gfin)
    return out.reshape(SEQ, D_MODEL)


def kernel(x, g_mix, w_in, w_conv, w_fourier, w_pool, pool_scale, w_out, g_ffn, w1, w3, w2, g_final):
    tabs = {k: jnp.asarray(v) for k, v in _tables().items()}
    wout, win, gf_col = _fold_weights(tabs["chan"], w_fourier, w_pool, pool_scale, w_out, w_in, g_mix, g_ffn)
    f_stage1 = tabs["f_stage1"].astype(BF16)

    xs = x.reshape(SEQ, D_MODEL)
    for l in range(DEPTH):
        yac, z, w1b, w3b, w2b = _mix_in(l, xs, win, w_conv, gf_col, w1, w3, w2)
        t = _dft_a(f_stage1, z)
        xs = _mix_out(l, xs, yac, t, tabs, wout, w1b, w3b, w2b, g_final[None, :],
                      final_norm=(l == DEPTH - 1))
    return xs.reshape(1, SEQ, D_MODEL)
```

```python
import functools

import numpy as np
import jax
import jax.numpy as jnp
from jax import lax
from jax.experimental import pallas as pl
from jax.experimental.pallas import tpu as pltpu

D_MODEL = 1024
SEQ = 16384
DEPTH = 2
HEAD_DIM = 64
CONV_DIM = 384
FOURIER_HEADS = 4
FOURIER_DIM = 256
POOL_WINDOWS = (2, 4, 8, 16)
POOL_GROUP_DIM = 96
POOL_DIM = 384
IN_PROJ_DIM = 3 * CONV_DIM + FOURIER_DIM + POOL_DIM
D_FF = 2816
EPS = 1e-6

RADIX = 128
YAC_DIM = CONV_DIM + POOL_DIM
Z_DIM = 2 * FOURIER_DIM
LANES = 128
SUBLANES = 8
DFT_COLS = 16

TM_IN = 1024
HALO = 8
assert all(w // 2 == 2 ** k for k, w in enumerate(POOL_WINDOWS)) and POOL_WINDOWS[-1] // 2 <= HALO
OUT_SUBTILES = 2
D_ROWS = RADIX // OUT_SUBTILES
TM_OUT = D_ROWS * SUBLANES
MXU_DIM = 256
FF_CHUNKS = (1536, 1280)
assert sum(FF_CHUNKS) == D_FF and all(w % MXU_DIM == 0 for w in FF_CHUNKS)
VMEM_LIMIT = 56 * 1024 * 1024

F32 = jnp.float32
BF16 = jnp.bfloat16


def _dot(a, b):
    return jnp.dot(a, b, preferred_element_type=F32)


def _unit_rms(v):
    ms = jnp.mean(v * v, axis=-1, keepdims=True)
    return v * lax.rsqrt(ms + EPS)


def _tables():
    r = np.arange(RADIX, dtype=np.float64)
    ang128 = 2.0 * np.pi * np.outer(r, r) / RADIX
    angs = 2.0 * np.pi * np.outer(r, r) / SEQ
    f_stage1 = np.concatenate([np.cos(ang128), -np.sin(ang128)], axis=0)
    e = np.arange(HEAD_DIM, dtype=np.float64)
    angc = 2.0 * np.pi * np.outer(e, e) / HEAD_DIM
    norm = 1.0 / np.sqrt(float(SEQ) * HEAD_DIM)
    eye = np.eye(FOURIER_HEADS)
    cbd = np.kron(eye, np.cos(angc)) * norm
    sbd = np.kron(eye, -np.sin(angc)) * norm
    return dict(
        f_stage1=f_stage1.astype(np.float32),
        cphi=np.cos(ang128).astype(np.float32), sphi=np.sin(ang128).astype(np.float32),
        cth=np.cos(angs).astype(np.float32), sth=np.sin(angs).astype(np.float32),
        chan=np.stack([cbd, sbd]).astype(np.float32),
    )


WIN_CH = 0
WIN_P = WIN_CH + 2 * CONV_DIM
WIN_F = WIN_P + POOL_DIM
WIN_B = WIN_F + Z_DIM
WIN_DIM = WIN_B + CONV_DIM
assert WIN_DIM % (2 * MXU_DIM) == 0
FOLD_PARTS = 4
FOLD_ROWS = D_MODEL // FOLD_PARTS


def _dot_3pass(a, b):
    a_hi = a.astype(BF16)
    b_hi = b.astype(BF16)
    a_lo = (a - a_hi.astype(F32)).astype(BF16)
    b_lo = (b - b_hi.astype(F32)).astype(BF16)
    return _dot(a_hi, b_hi) + (_dot(a_hi, b_lo) + _dot(a_lo, b_hi))


def _row_as_column(row):
    diagonal = (lax.broadcasted_iota(jnp.int32, (LANES, LANES), 0)
                == lax.broadcasted_iota(jnp.int32, (LANES, LANES), 1))
    blocks = [jnp.sum(jnp.where(diagonal, row[:, k:k + LANES], 0.0), axis=1, keepdims=True)
              for k in range(0, row.shape[1], LANES)]
    return jnp.concatenate(blocks, axis=0)


def _fold_kernel(chan_ref, wf_ref, wp_ref, ps_ref, wout_ref, win_ref, gmix_ref, gffn_ref,
                 woe_ref, wine_ref, gffn_col_ref, gain_ref, ab_ref, wfbd_ref, wpbd_ref):
    layer = pl.program_id(0)
    part = pl.program_id(1)

    @pl.when(part == 0)
    def _():
        gain_ref[...] = _row_as_column(gmix_ref[pl.ds(layer, 1), :])
        gffn_col_ref[...] = _row_as_column(gffn_ref[pl.ds(layer, 1), :])
        wfbd_ref[...] = jnp.zeros(wfbd_ref.shape, F32)
        for hd in range(FOURIER_HEADS):
            lo = hd * HEAD_DIM
            wfbd_ref[lo:lo + HEAD_DIM, lo:lo + HEAD_DIM] = wf_ref[hd]
        wpbd_ref[...] = jnp.zeros(wpbd_ref.shape, F32)
        for grp in range(len(POOL_WINDOWS)):
            lo = grp * POOL_GROUP_DIM
            wpbd_ref[lo:lo + POOL_GROUP_DIM, lo:lo + POOL_GROUP_DIM] = wp_ref[grp]
        wf = wfbd_ref[...]
        ab_ref[:, 0:FOURIER_DIM] = _dot_3pass(chan_ref[0], wf)
        ab_ref[:, FOURIER_DIM:Z_DIM] = _dot_3pass(chan_ref[1], wf)

    mixed = CONV_DIM + FOURIER_DIM
    woe_ref[0:mixed, :] = wout_ref[0:mixed, :].astype(BF16)
    woe_ref[mixed:D_MODEL, :] = _dot_3pass(wpbd_ref[...] * ps_ref[pl.ds(layer, 1), :],
                                           wout_ref[mixed:D_MODEL, :]).astype(BF16)
    c0 = CONV_DIM
    gain = gain_ref[pl.ds(pl.multiple_of(part * FOLD_ROWS, FOLD_ROWS), FOLD_ROWS), :]
    wine_ref[:, WIN_CH:WIN_CH + 2 * c0] = (win_ref[:, c0:3 * c0] * gain).astype(BF16)
    wine_ref[:, WIN_P:WIN_P + POOL_DIM] = (win_ref[:, 3 * c0 + FOURIER_DIM:IN_PROJ_DIM] * gain).astype(BF16)
    wine_ref[:, WIN_B:WIN_B + c0] = (win_ref[:, 0:c0] * gain).astype(BF16)
    wfour = win_ref[:, 3 * c0:3 * c0 + FOURIER_DIM] * gain
    wine_ref[:, WIN_F:WIN_F + Z_DIM] = _dot_3pass(wfour, ab_ref[...]).astype(BF16)


def _fold_weights(chan, w_fourier, w_pool, pool_scale, w_out, w_in, g_mix, g_ffn):
    lay = lambda l, q: (l, 0, 0)
    lay4 = lambda l, q: (l, 0, 0, 0)
    whole = lambda l, q: (0, 0)
    rows = lambda l, q: (l, q, 0)
    cols = lambda l, q: (l, 0, q)
    return pl.pallas_call(
        _fold_kernel,
        grid=(DEPTH, FOLD_PARTS),
        in_specs=[pl.BlockSpec((2, FOURIER_DIM, FOURIER_DIM), lambda l, q: (0, 0, 0)),
                  pl.BlockSpec((None, FOURIER_HEADS, HEAD_DIM, HEAD_DIM), lay4),
                  pl.BlockSpec((None, len(POOL_WINDOWS), POOL_GROUP_DIM, POOL_GROUP_DIM), lay4),
                  pl.BlockSpec((DEPTH, POOL_DIM), whole),
                  pl.BlockSpec((None, D_MODEL, FOLD_ROWS), cols),
                  pl.BlockSpec((None, FOLD_ROWS, IN_PROJ_DIM), rows),
                  pl.BlockSpec((DEPTH, D_MODEL), whole),
                  pl.BlockSpec((DEPTH, D_MODEL), whole)],
        out_specs=[pl.BlockSpec((None, D_MODEL, FOLD_ROWS), cols),
                   pl.BlockSpec((None, FOLD_ROWS, WIN_DIM), rows),
                   pl.BlockSpec((None, D_MODEL, 1), lay)],
        out_shape=[jax.ShapeDtypeStruct((DEPTH, D_MODEL, D_MODEL), BF16),
                   jax.ShapeDtypeStruct((DEPTH, D_MODEL, WIN_DIM), BF16),
                   jax.ShapeDtypeStruct((DEPTH, D_MODEL, 1), F32)],
        scratch_shapes=[pltpu.VMEM((D_MODEL, 1), F32), pltpu.VMEM((FOURIER_DIM, Z_DIM), F32),
                        pltpu.VMEM((FOURIER_DIM, FOURIER_DIM), F32), pltpu.VMEM((POOL_DIM, POOL_DIM), F32)],
        compiler_params=pltpu.CompilerParams(dimension_semantics=("arbitrary", "arbitrary")),
        name="fold_weights",
    )(chan, w_fourier, w_pool, pool_scale, w_out, w_in, g_mix, g_ffn)


def _window_sums(p, rows, max_half):
    sums = {1: p + pltpu.roll(p, 1, 0)}
    h = 1
    while h < max_half:
        sums[2 * h] = pltpu.roll(sums[h], h, 0) + pltpu.roll(sums[h], rows - h, 0)
        h *= 2
    return sums


def _mix_in_kernel(x_ref, xprev_ref, xnext_ref, win_ref, wconv_ref, gffn_ref, w1_ref, w3_ref, w2_ref,
                   yac_ref, z_ref, w1b_ref, w3b_ref, w2b_ref, xe_ref):
    gffn = gffn_ref[...]
    w1b_ref[...] = (w1_ref[...] * gffn).astype(BF16)
    w3b_ref[...] = (w3_ref[...] * gffn).astype(BF16)
    w2b_ref[...] = w2_ref[...].astype(BF16)

    i = pl.program_id(0)
    n = pl.num_programs(0)
    rows = TM_IN + 2 * HALO
    prev = jnp.where(i > 0, _unit_rms(xprev_ref[...]), 0.0)
    nxt = jnp.where(i < n - 1, _unit_rms(xnext_ref[...]), 0.0)
    xe_ref[0:TM_IN, :] = _unit_rms(x_ref[...]).astype(BF16)
    xe_ref[TM_IN:rows, :] = jnp.concatenate([nxt, prev], axis=0).astype(BF16)

    z = _dot(xe_ref[...], win_ref[...])
    c0 = CONV_DIM
    tile = slice(0, TM_IN)
    zp = z[:, WIN_P:WIN_P + POOL_DIM]
    z_ref[...] = z[tile, WIN_F:WIN_F + Z_DIM].astype(BF16)

    u = z[:, WIN_CH:WIN_CH + c0] * z[:, WIN_CH + c0:WIN_CH + 2 * c0]
    conv = (wconv_ref[0:1, :] * pltpu.roll(u, 1, 0) + wconv_ref[1:2, :] * u
            + wconv_ref[2:3, :] * pltpu.roll(u, rows - 1, 0))
    yac_ref[:, 0:c0] = z[tile, WIN_B:WIN_B + c0] * conv[tile]

    lane_tiles = []
    for lo in range(0, POOL_DIM, LANES):
        p = zp[:, lo:lo + LANES]
        groups = range(lo // POOL_GROUP_DIM, (lo + LANES - 1) // POOL_GROUP_DIM + 1)
        sums = _window_sums(p, rows, POOL_WINDOWS[groups[-1]] // 2)
        lane = lo + lax.broadcasted_iota(jnp.int32, (1, LANES), 1)
        wsum = sums[POOL_WINDOWS[groups[-1]] // 2][tile]
        half = jnp.full((1, LANES), POOL_WINDOWS[groups[-1]] // 2, jnp.int32)
        for grp in reversed(groups[:-1]):
            inside = lane < (grp + 1) * POOL_GROUP_DIM
            wsum = jnp.where(inside, sums[POOL_WINDOWS[grp] // 2][tile], wsum)
            half = jnp.where(inside, POOL_WINDOWS[grp] // 2, half)
        cols = slice(c0 + lo, c0 + lo + LANES)
        yac_ref[:, cols] = wsum * (1.0 / (2 * half).astype(F32)) - p[tile]
        lane_tiles.append((cols, wsum, half, p))

    def clipped_window_rows(local, first_t):
        t = first_t + lax.broadcasted_iota(jnp.int32, (HALO, LANES), 0)
        for cols, wsum, half, p in lane_tiles:
            cnt = (jnp.minimum(t + half, SEQ) - jnp.maximum(t - half, 0)).astype(F32)
            yac_ref[local, cols] = wsum[local] / cnt - p[local]

    @pl.when(i == 0)
    def _():
        clipped_window_rows(slice(0, HALO), 0)

    @pl.when(i == n - 1)
    def _():
        clipped_window_rows(slice(TM_IN - HALO, TM_IN), SEQ - HALO)


def _mix_in(layer, x, win, wconv, gffn_col, w1, w3, w2):
    n = SEQ // TM_IN
    hb = TM_IN // HALO
    lay = lambda i: (layer, 0, 0)
    w_rows = lambda i: (layer, i, 0)
    cast_rows = lambda i: (i, 0)
    return pl.pallas_call(
        _mix_in_kernel,
        grid=(n,),
        in_specs=[
            pl.BlockSpec((TM_IN, D_MODEL), lambda i: (i, 0)),
            pl.BlockSpec((HALO, D_MODEL), lambda i: (jnp.maximum(i * hb - 1, 0), 0)),
            pl.BlockSpec((HALO, D_MODEL), lambda i: (jnp.minimum((i + 1) * hb, SEQ // HALO - 1), 0)),
            pl.BlockSpec((None, D_MODEL, WIN_DIM), lay, pipeline_mode=pl.Buffered(1)),
            pl.BlockSpec((None, 3, CONV_DIM), lay),
            pl.BlockSpec((None, D_MODEL // n, 1), w_rows),
            pl.BlockSpec((None, D_MODEL // n, D_FF), w_rows),
            pl.BlockSpec((None, D_MODEL // n, D_FF), w_rows),
            pl.BlockSpec((None, D_FF // n, D_MODEL), w_rows),
        ],
        out_specs=[pl.BlockSpec((TM_IN, YAC_DIM), lambda i: (i, 0)),
                   pl.BlockSpec((TM_IN, Z_DIM), lambda i: (i, 0)),
                   pl.BlockSpec((D_MODEL // n, D_FF), cast_rows),
                   pl.BlockSpec((D_MODEL // n, D_FF), cast_rows),
                   pl.BlockSpec((D_FF // n, D_MODEL), cast_rows)],
        out_shape=[jax.ShapeDtypeStruct((SEQ, YAC_DIM), F32),
                   jax.ShapeDtypeStruct((SEQ, Z_DIM), BF16),
                   jax.ShapeDtypeStruct((D_MODEL, D_FF), BF16),
                   jax.ShapeDtypeStruct((D_MODEL, D_FF), BF16),
                   jax.ShapeDtypeStruct((D_FF, D_MODEL), BF16)],
        scratch_shapes=[pltpu.VMEM((TM_IN + 2 * HALO, D_MODEL), BF16)],
        compiler_params=pltpu.CompilerParams(dimension_semantics=("arbitrary",),
                                             vmem_limit_bytes=VMEM_LIMIT),
        name="mix_in",
    )(x, x, x, win, wconv, gffn_col, w1, w3, w2)


def _dft_a_kernel(f_ref, z_ref, t_ref, zslab_ref, tslab_ref):
    h = FOURIER_DIM
    f = f_ref[...]
    zslab_ref[...] = jnp.swapaxes(z_ref[...].astype(F32), 0, 1)
    for j in range(DFT_COLS):
        res = _dot(f, zslab_ref[j].astype(BF16))
        tslab_ref[j, :, 0:h] = res[0:RADIX, 0:h] - res[RADIX:2 * RADIX, h:Z_DIM]
        tslab_ref[j, :, h:Z_DIM] = res[0:RADIX, h:Z_DIM] + res[RADIX:2 * RADIX, 0:h]
    t_ref[...] = jnp.swapaxes(tslab_ref[...], 0, 1).astype(BF16)


def _dft_a(f_stage1, z):
    z3 = z.reshape(RADIX, RADIX, Z_DIM)
    block = (RADIX, DFT_COLS, Z_DIM)
    return pl.pallas_call(
        _dft_a_kernel,
        grid=(RADIX // DFT_COLS,),
        in_specs=[pl.BlockSpec((2 * RADIX, RADIX), lambda t: (0, 0)),
                  pl.BlockSpec(block, lambda t: (0, t, 0))],
        out_specs=pl.BlockSpec(block, lambda t: (0, t, 0)),
        out_shape=jax.ShapeDtypeStruct((RADIX, RADIX, Z_DIM), BF16),
        scratch_shapes=[pltpu.VMEM((DFT_COLS, RADIX, Z_DIM), F32), pltpu.VMEM((DFT_COLS, RADIX, Z_DIM), F32)],
        compiler_params=pltpu.CompilerParams(dimension_semantics=("arbitrary",)),
        name="dft_a",
    )(f_stage1, z3)


def _mix_out_kernel(x_ref, yac_ref, t_ref, cphi_ref, sphi_ref, cth_ref, sth_ref, wout_ref,
                    w1_ref, w3_ref, w2_ref, gfin_ref, out_ref, yb_ref, ybuf, *, final_norm):
    chunk = pl.program_id(0)
    c0 = CONV_DIM
    h = FOURIER_DIM

    for sub in range(OUT_SUBTILES):
        local = slice(sub * D_ROWS, (sub + 1) * D_ROWS)
        cphi = cphi_ref[local, :]
        sphi = sphi_ref[local, :]
        for j in range(SUBLANES):
            c = chunk * SUBLANES + j
            cth = cth_ref[pl.ds(c, 1), :]
            sth = sth_ref[pl.ds(c, 1), :]
            mcos = (cphi * cth - sphi * sth).astype(BF16)
            msin = (sphi * cth + cphi * sth).astype(BF16)
            tj = t_ref[j]
            yb_ref[sub, :, j, :] = _dot(jnp.concatenate([mcos, msin], axis=1),
                                        jnp.concatenate([tj[:, 0:h], tj[:, h:Z_DIM]], axis=0))

        yac = yac_ref[local].reshape(TM_OUT, YAC_DIM)
        yb = yb_ref[sub].reshape(TM_OUT, h)
        ybuf[sub, :, 0:c0] = yac[:, 0:c0].astype(BF16)
        ybuf[sub, :, c0:c0 + h] = yb.astype(BF16)
        ybuf[sub, :, c0 + h:D_MODEL] = yac[:, c0:YAC_DIM].astype(BF16)

        x1 = x_ref[local].reshape(TM_OUT, D_MODEL) + _dot(ybuf[sub], wout_ref[...])
        xg = x1.astype(BF16)
        r = lax.rsqrt(jnp.mean(x1 * x1, axis=-1, keepdims=True) + EPS)
        acc = None
        start = 0
        for width in FF_CHUNKS:
            cols = slice(start, start + width)
            start += width
            gate = _dot(xg, w1_ref[:, cols]) * r
            up = _dot(xg, w3_ref[:, cols])
            hid = (gate * (1.0 / (1.0 + jnp.exp(-gate))) * up).astype(BF16)
            ffn = _dot(hid, w2_ref[cols, :])
            acc = ffn if acc is None else acc + ffn
        x2 = x1 + acc * r
        if final_norm:
            x2 = _unit_rms(x2) * gfin_ref[...]
        out_ref[local] = x2.reshape(D_ROWS, SUBLANES, D_MODEL)


def _mix_out(layer, x, yac, t, tabs, wout, w1, w3, w2, gfin, final_norm):
    x3 = x.reshape(RADIX, RADIX, D_MODEL)
    yac3 = yac.reshape(RADIX, RADIX, YAC_DIM)
    const = lambda s: (0, 0)
    lay = lambda s: (layer, 0, 0)
    tile = lambda s: (0, s, 0)
    once = dict(pipeline_mode=pl.Buffered(1))
    out = pl.pallas_call(
        functools.partial(_mix_out_kernel, final_norm=final_norm),
        grid=(RADIX // SUBLANES,),
        in_specs=[
            pl.BlockSpec((RADIX, SUBLANES, D_MODEL), tile),
            pl.BlockSpec((RADIX, SUBLANES, YAC_DIM), tile),
            pl.BlockSpec((SUBLANES, RADIX, Z_DIM), lambda s: (s, 0, 0)),
            pl.BlockSpec((RADIX, RADIX), const),
            pl.BlockSpec((RADIX, RADIX), const),
            pl.BlockSpec((RADIX, RADIX), const),
            pl.BlockSpec((RADIX, RADIX), const),
            pl.BlockSpec((None, D_MODEL, D_MODEL), lay, **once),
            pl.BlockSpec((D_MODEL, D_FF), const, **once),
            pl.BlockSpec((D_MODEL, D_FF), const, **once),
            pl.BlockSpec((D_FF, D_MODEL), const, **once),
            pl.BlockSpec((1, D_MODEL), const),
        ],
        out_specs=pl.BlockSpec((RADIX, SUBLANES, D_MODEL), tile),
        out_shape=jax.ShapeDtypeStruct((RADIX, RADIX, D_MODEL), F32),
        scratch_shapes=[pltpu.VMEM((OUT_SUBTILES, D_ROWS, SUBLANES, FOURIER_DIM), F32),
                        pltpu.VMEM((OUT_SUBTILES, TM_OUT, D_MODEL), BF16)],
        compiler_params=pltpu.CompilerParams(dimension_semantics=("arbitrary",),
                                             vmem_limit_bytes=VMEM_LIMIT),
        name="mix_out",
    )(x3, yac3, t, tabs["cphi"], tabs["sphi"], tabs["cth"], tabs["sth"], wout, w1, w3, w2, gfin)
    return out.reshape(SEQ, D_MODEL)


def kernel(x, g_mix, w_in, w_conv, w_fourier, w_pool, pool_scale, w_out, g_ffn, w1, w3, w2, g_final):
    tabs = {k: jnp.asarray(v) for k, v in _tables().items()}
    wout, win, gf_col = _fold_weights(tabs["chan"], w_fourier, w_pool, pool_scale, w_out, w_in, g_mix, g_ffn)
    f_stage1 = tabs["f_stage1"].astype(BF16)

    xs = x.reshape(SEQ, D_MODEL)
    for l in range(DEPTH):
        yac, z, w1b, w3b, w2b = _mix_in(l, xs, win, w_conv, gf_col, w1, w3, w2)
        t = _dft_a(f_stage1, z)
        xs = _mix_out(l, xs, yac, t, tabs, wout, w1b, w3b, w2b, g_final[None, :],
                      final_norm=(l == DEPTH - 1))
    return xs.reshape(1, SEQ, D_MODEL)
```

```python
import functools

import numpy as np
import jax
import jax.numpy as jnp
from jax import lax
from jax.experimental import pallas as pl
from jax.experimental.pallas import tpu as pltpu

D_MODEL = 1024
SEQ = 16384
DEPTH = 2
HEAD_DIM = 64
CONV_DIM = 384
FOURIER_HEADS = 4
FOURIER_DIM = 256
POOL_WINDOWS = (2, 4, 8, 16)
POOL_GROUP_DIM = 96
POOL_DIM = 384
IN_PROJ_DIM = 3 * CONV_DIM + FOURIER_DIM + POOL_DIM
D_FF = 2816
EPS = 1e-6

RADIX = 128
YAC_DIM = CONV_DIM + POOL_DIM
Z_DIM = 2 * FOURIER_DIM
LANES = 128
SUBLANES = 8
DFT_COLS = 16

TM_IN = 1024
HALO = 8
assert all(w // 2 == 2 ** k for k, w in enumerate(POOL_WINDOWS)) and POOL_WINDOWS[-1] // 2 <= HALO
OUT_SUBTILES = 2
D_ROWS = RADIX // OUT_SUBTILES
TM_OUT = D_ROWS * SUBLANES
MXU_DIM = 256
FF_CHUNKS = (1536, 1280)
assert sum(FF_CHUNKS) == D_FF and all(w % MXU_DIM == 0 for w in FF_CHUNKS)
VMEM_LIMIT = 56 * 1024 * 1024

F32 = jnp.float32
BF16 = jnp.bfloat16


def _dot(a, b):
    return jnp.dot(a, b, preferred_element_type=F32)


def _unit_rms(v):
    ms = jnp.mean(v * v, axis=-1, keepdims=True)
    return v * lax.rsqrt(ms + EPS)


def _tables():
    r = np.arange(RADIX, dtype=np.float64)
    ang128 = 2.0 * np.pi * np.outer(r, r) / RADIX
    angs = 2.0 * np.pi * np.outer(r, r) / SEQ
    f_stage1 = np.concatenate([np.cos(ang128), -np.sin(ang128)], axis=0)
    e = np.arange(HEAD_DIM, dtype=np.float64)
    angc = 2.0 * np.pi * np.outer(e, e) / HEAD_DIM
    norm = 1.0 / np.sqrt(float(SEQ) * HEAD_DIM)
    eye = np.eye(FOURIER_HEADS)
    cbd = np.kron(eye, np.cos(angc)) * norm
    sbd = np.kron(eye, -np.sin(angc)) * norm
    return dict(
        f_stage1=f_stage1.astype(np.float32),
        cphi=np.cos(ang128).astype(np.float32), sphi=np.sin(ang128).astype(np.float32),
        cth=np.cos(angs).astype(np.float32), sth=np.sin(angs).astype(np.float32),
        chan=np.stack([cbd, sbd]).astype(np.float32),
    )


WIN_CH = 0
WIN_P = WIN_CH + 2 * CONV_DIM
WIN_F = WIN_P + POOL_DIM
WIN_B = WIN_F + Z_DIM
WIN_DIM = WIN_B + CONV_DIM
assert WIN_DIM % (2 * MXU_DIM) == 0
FOLD_PARTS = 4
FOLD_ROWS = D_MODEL // FOLD_PARTS


def _dot_3pass(a, b):
    a_hi = a.astype(BF16)
    b_hi = b.astype(BF16)
    a_lo = (a - a_hi.astype(F32)).astype(BF16)
    b_lo = (b - b_hi.astype(F32)).astype(BF16)
    return _dot(a_hi, b_hi) + (_dot(a_hi, b_lo) + _dot(a_lo, b_hi))


def _row_as_column(row):
    diagonal = (lax.broadcasted_iota(jnp.int32, (LANES, LANES), 0)
                == lax.broadcasted_iota(jnp.int32, (LANES, LANES), 1))
    blocks = [jnp.sum(jnp.where(diagonal, row[:, k:k + LANES], 0.0), axis=1, keepdims=True)
              for k in range(0, row.shape[1], LANES)]
    return jnp.concatenate(blocks, axis=0)


def _fold_kernel(chan_ref, wf_ref, wp_ref, ps_ref, wout_ref, win_ref, gmix_ref, gffn_ref,
                 woe_ref, wine_ref, gffn_col_ref, gain_ref, ab_ref, wfbd_ref, wpbd_ref):
    layer = pl.program_id(0)
    part = pl.program_id(1)

    @pl.when(part == 0)
    def _():
        gain_ref[...] = _row_as_column(gmix_ref[pl.ds(layer, 1), :])
        gffn_col_ref[...] = _row_as_column(gffn_ref[pl.ds(layer, 1), :])
        wfbd_ref[...] = jnp.zeros(wfbd_ref.shape, F32)
        for hd in range(FOURIER_HEADS):
            lo = hd * HEAD_DIM
            wfbd_ref[lo:lo + HEAD_DIM, lo:lo + HEAD_DIM] = wf_ref[hd]
        wpbd_ref[...] = jnp.zeros(wpbd_ref.shape, F32)
        for grp in range(len(POOL_WINDOWS)):
            lo = grp * POOL_GROUP_DIM
            wpbd_ref[lo:lo + POOL_GROUP_DIM, lo:lo + POOL_GROUP_DIM] = wp_ref[grp]
        wf = wfbd_ref[...]
        ab_ref[:, 0:FOURIER_DIM] = _dot_3pass(chan_ref[0], wf)
        ab_ref[:, FOURIER_DIM:Z_DIM] = _dot_3pass(chan_ref[1], wf)

    mixed = CONV_DIM + FOURIER_DIM
    woe_ref[0:mixed, :] = wout_ref[0:mixed, :].astype(BF16)
    woe_ref[mixed:D_MODEL, :] = _dot_3pass(wpbd_ref[...] * ps_ref[pl.ds(layer, 1), :],
                                           wout_ref[mixed:D_MODEL, :]).astype(BF16)
    c0 = CONV_DIM
    gain = gain_ref[pl.ds(pl.multiple_of(part * FOLD_ROWS, FOLD_ROWS), FOLD_ROWS), :]
    wine_ref[:, WIN_CH:WIN_CH + 2 * c0] = (win_ref[:, c0:3 * c0] * gain).astype(BF16)
    wine_ref[:, WIN_P:WIN_P + POOL_DIM] = (win_ref[:, 3 * c0 + FOURIER_DIM:IN_PROJ_DIM] * gain).astype(BF16)
    wine_ref[:, WIN_B:WIN_B + c0] = (win_ref[:, 0:c0] * gain).astype(BF16)
    wfour = win_ref[:, 3 * c0:3 * c0 + FOURIER_DIM] * gain
    wine_ref[:, WIN_F:WIN_F + Z_DIM] = _dot_3pass(wfour, ab_ref[...]).astype(BF16)


def _fold_weights(chan, w_fourier, w_pool, pool_scale, w_out, w_in, g_mix, g_ffn):
    lay = lambda l, q: (l, 0, 0)
    lay4 = lambda l, q: (l, 0, 0, 0)
    whole = lambda l, q: (0, 0)
    rows = lambda l, q: (l, q, 0)
    cols = lambda l, q: (l, 0, q)
    return pl.pallas_call(
        _fold_kernel,
        grid=(DEPTH, FOLD_PARTS),
        in_specs=[pl.BlockSpec((2, FOURIER_DIM, FOURIER_DIM), lambda l, q: (0, 0, 0)),
                  pl.BlockSpec((None, FOURIER_HEADS, HEAD_DIM, HEAD_DIM), lay4),
                  pl.BlockSpec((None, len(POOL_WINDOWS), POOL_GROUP_DIM, POOL_GROUP_DIM), lay4),
                  pl.BlockSpec((DEPTH, POOL_DIM), whole),
                  pl.BlockSpec((None, D_MODEL, FOLD_ROWS), cols),
                  pl.BlockSpec((None, FOLD_ROWS, IN_PROJ_DIM), rows),
                  pl.BlockSpec((DEPTH, D_MODEL), whole),
                  pl.BlockSpec((DEPTH, D_MODEL), whole)],
        out_specs=[pl.BlockSpec((None, D_MODEL, FOLD_ROWS), cols),
                   pl.BlockSpec((None, FOLD_ROWS, WIN_DIM), rows),
                   pl.BlockSpec((None, D_MODEL, 1), lay)],
        out_shape=[jax.ShapeDtypeStruct((DEPTH, D_MODEL, D_MODEL), BF16),
                   jax.ShapeDtypeStruct((DEPTH, D_MODEL, WIN_DIM), BF16),
                   jax.ShapeDtypeStruct((DEPTH, D_MODEL, 1), F32)],
        scratch_shapes=[pltpu.VMEM((D_MODEL, 1), F32), pltpu.VMEM((FOURIER_DIM, Z_DIM), F32),
                        pltpu.VMEM((FOURIER_DIM, FOURIER_DIM), F32), pltpu.VMEM((POOL_DIM, POOL_DIM), F32)],
        compiler_params=pltpu.CompilerParams(dimension_semantics=("arbitrary", "arbitrary")),
        name="fold_weights",
    )(chan, w_fourier, w_pool, pool_scale, w_out, w_in, g_mix, g_ffn)


def _window_sums(p, rows, max_half):
    sums = {1: p + pltpu.roll(p, 1, 0)}
    h = 1
    while h < max_half:
        sums[2 * h] = pltpu.roll(sums[h], h, 0) + pltpu.roll(sums[h], rows - h, 0)
        h *= 2
    return sums


def _mix_in_kernel(x_ref, xprev_ref, xnext_ref, win_ref, wconv_ref, gffn_ref, w1_ref, w3_ref, w2_ref,
                   yac_ref, z_ref, w13b_ref, w2b_ref, xe_ref):
    gffn = gffn_ref[...]
    start = 0
    for width in FF_CHUNKS:
        w13b_ref[:, 2 * start:2 * start + width] = (w1_ref[:, start:start + width] * gffn).astype(BF16)
        w13b_ref[:, 2 * start + width:2 * (start + width)] = (w3_ref[:, start:start + width] * gffn).astype(BF16)
        start += width
    w2b_ref[...] = w2_ref[...].astype(BF16)

    i = pl.program_id(0)
    n = pl.num_programs(0)
    rows = TM_IN + 2 * HALO
    prev = jnp.where(i > 0, _unit_rms(xprev_ref[...]), 0.0)
    nxt = jnp.where(i < n - 1, _unit_rms(xnext_ref[...]), 0.0)
    xe_ref[0:TM_IN, :] = _unit_rms(x_ref[...]).astype(BF16)
    xe_ref[TM_IN:rows, :] = jnp.concatenate([nxt, prev], axis=0).astype(BF16)

    z = _dot(xe_ref[...], win_ref[...])
    c0 = CONV_DIM
    tile = slice(0, TM_IN)
    zp = z[:, WIN_P:WIN_P + POOL_DIM]
    z_ref[...] = z[tile, WIN_F:WIN_F + Z_DIM].astype(BF16)

    u = z[:, WIN_CH:WIN_CH + c0] * z[:, WIN_CH + c0:WIN_CH + 2 * c0]
    conv = (wconv_ref[0:1, :] * pltpu.roll(u, 1, 0) + wconv_ref[1:2, :] * u
            + wconv_ref[2:3, :] * pltpu.roll(u, rows - 1, 0))
    yac_ref[:, 0:c0] = z[tile, WIN_B:WIN_B + c0] * conv[tile]

    lane_tiles = []
    for lo in range(0, POOL_DIM, LANES):
        p = zp[:, lo:lo + LANES]
        groups = range(lo // POOL_GROUP_DIM, (lo + LANES - 1) // POOL_GROUP_DIM + 1)
        sums = _window_sums(p, rows, POOL_WINDOWS[groups[-1]] // 2)
        lane = lo + lax.broadcasted_iota(jnp.int32, (1, LANES), 1)
        wsum = sums[POOL_WINDOWS[groups[-1]] // 2][tile]
        half = jnp.full((1, LANES), POOL_WINDOWS[groups[-1]] // 2, jnp.int32)
        for grp in reversed(groups[:-1]):
            inside = lane < (grp + 1) * POOL_GROUP_DIM
            wsum = jnp.where(inside, sums[POOL_WINDOWS[grp] // 2][tile], wsum)
            half = jnp.where(inside, POOL_WINDOWS[grp] // 2, half)
        cols = slice(c0 + lo, c0 + lo + LANES)
        yac_ref[:, cols] = wsum * (1.0 / (2 * half).astype(F32)) - p[tile]
        lane_tiles.append((cols, wsum, half, p))

    def clipped_window_rows(local, first_t):
        t = first_t + lax.broadcasted_iota(jnp.int32, (HALO, LANES), 0)
        for cols, wsum, half, p in lane_tiles:
            cnt = (jnp.minimum(t + half, SEQ) - jnp.maximum(t - half, 0)).astype(F32)
            yac_ref[local, cols] = wsum[local] / cnt - p[local]

    @pl.when(i == 0)
    def _():
        clipped_window_rows(slice(0, HALO), 0)

    @pl.when(i == n - 1)
    def _():
        clipped_window_rows(slice(TM_IN - HALO, TM_IN), SEQ - HALO)


def _mix_in(layer, x, win, wconv, gffn_col, w1, w3, w2):
    n = SEQ // TM_IN
    hb = TM_IN // HALO
    lay = lambda i: (layer, 0, 0)
    w_rows = lambda i: (layer, i, 0)
    cast_rows = lambda i: (i, 0)
    return pl.pallas_call(
        _mix_in_kernel,
        grid=(n,),
        in_specs=[
            pl.BlockSpec((TM_IN, D_MODEL), lambda i: (i, 0)),
            pl.BlockSpec((HALO, D_MODEL), lambda i: (jnp.maximum(i * hb - 1, 0), 0)),
            pl.BlockSpec((HALO, D_MODEL), lambda i: (jnp.minimum((i + 1) * hb, SEQ // HALO - 1), 0)),
            pl.BlockSpec((None, D_MODEL, WIN_DIM), lay, pipeline_mode=pl.Buffered(1)),
            pl.BlockSpec((None, 3, CONV_DIM), lay),
            pl.BlockSpec((None, D_MODEL // n, 1), w_rows),
            pl.BlockSpec((None, D_MODEL // n, D_FF), w_rows),
            pl.BlockSpec((None, D_MODEL // n, D_FF), w_rows),
            pl.BlockSpec((None, D_FF // n, D_MODEL), w_rows),
        ],
        out_specs=[pl.BlockSpec((TM_IN, YAC_DIM), lambda i: (i, 0)),
                   pl.BlockSpec((TM_IN, Z_DIM), lambda i: (i, 0)),
                   pl.BlockSpec((D_MODEL // n, 2 * D_FF), cast_rows),
                   pl.BlockSpec((D_FF // n, D_MODEL), cast_rows)],
        out_shape=[jax.ShapeDtypeStruct((SEQ, YAC_DIM), F32),
                   jax.ShapeDtypeStruct((SEQ, Z_DIM), BF16),
                   jax.ShapeDtypeStruct((D_MODEL, 2 * D_FF), BF16),
                   jax.ShapeDtypeStruct((D_FF, D_MODEL), BF16)],
        scratch_shapes=[pltpu.VMEM((TM_IN + 2 * HALO, D_MODEL), BF16)],
        compiler_params=pltpu.CompilerParams(dimension_semantics=("arbitrary",),
                                             vmem_limit_bytes=VMEM_LIMIT),
        name="mix_in",
    )(x, x, x, win, wconv, gffn_col, w1, w3, w2)


def _dft_a_kernel(f_ref, z_ref, t_ref, zslab_ref, tslab_ref):
    h = FOURIER_DIM
    f = f_ref[...]
    zslab_ref[...] = jnp.swapaxes(z_ref[...].astype(F32), 0, 1)
    for j in range(DFT_COLS):
        res = _dot(f, zslab_ref[j].astype(BF16))
        tslab_ref[j, :, 0:h] = res[0:RADIX, 0:h] - res[RADIX:2 * RADIX, h:Z_DIM]
        tslab_ref[j, :, h:Z_DIM] = res[0:RADIX, h:Z_DIM] + res[RADIX:2 * RADIX, 0:h]
    t_ref[...] = jnp.swapaxes(tslab_ref[...], 0, 1).astype(BF16)


def _dft_a(f_stage1, z):
    z3 = z.reshape(RADIX, RADIX, Z_DIM)
    block = (RADIX, DFT_COLS, Z_DIM)
    return pl.pallas_call(
        _dft_a_kernel,
        grid=(RADIX // DFT_COLS,),
        in_specs=[pl.BlockSpec((2 * RADIX, RADIX), lambda t: (0, 0)),
                  pl.BlockSpec(block, lambda t: (0, t, 0))],
        out_specs=pl.BlockSpec(block, lambda t: (0, t, 0)),
        out_shape=jax.ShapeDtypeStruct((RADIX, RADIX, Z_DIM), BF16),
        scratch_shapes=[pltpu.VMEM((DFT_COLS, RADIX, Z_DIM), F32), pltpu.VMEM((DFT_COLS, RADIX, Z_DIM), F32)],
        compiler_params=pltpu.CompilerParams(dimension_semantics=("arbitrary",)),
        name="dft_a",
    )(f_stage1, z3)


def _mix_out_kernel(x_ref, yac_ref, t_ref, cphi_ref, sphi_ref, cth_ref, sth_ref, wout_ref,
                    w13_ref, w2_ref, gfin_ref, out_ref, yb_ref, ybuf, *, final_norm):
    chunk = pl.program_id(0)
    c0 = CONV_DIM
    h = FOURIER_DIM

    for sub in range(OUT_SUBTILES):
        local = slice(sub * D_ROWS, (sub + 1) * D_ROWS)
        cphi = cphi_ref[local, :]
        sphi = sphi_ref[local, :]
        for j in range(SUBLANES):
            c = chunk * SUBLANES + j
            cth = cth_ref[pl.ds(c, 1), :]
            sth = sth_ref[pl.ds(c, 1), :]
            mcos = (cphi * cth - sphi * sth).astype(BF16)
            msin = (sphi * cth + cphi * sth).astype(BF16)
            tj = t_ref[j]
            yb_ref[sub, :, j, :] = _dot(jnp.concatenate([mcos, msin], axis=1),
                                        jnp.concatenate([tj[:, 0:h], tj[:, h:Z_DIM]], axis=0))

        yac = yac_ref[local].reshape(TM_OUT, YAC_DIM)
        yb = yb_ref[sub].reshape(TM_OUT, h)
        ybuf[sub, :, 0:c0] = yac[:, 0:c0].astype(BF16)
        ybuf[sub, :, c0:c0 + h] = yb.astype(BF16)
        ybuf[sub, :, c0 + h:D_MODEL] = yac[:, c0:YAC_DIM].astype(BF16)

        x1 = x_ref[local].reshape(TM_OUT, D_MODEL) + _dot(ybuf[sub], wout_ref[...])
        xg = x1.astype(BF16)
        r = lax.rsqrt(jnp.mean(x1 * x1, axis=-1, keepdims=True) + EPS)
        acc = None
        start = 0
        for width in FF_CHUNKS:
            cols = slice(start, start + width)
            start += width
            both = _dot(xg, w13_ref[:, 2 * cols.start:2 * cols.stop])
            gate = both[:, 0:width] * r
            up = both[:, width:2 * width]
            hid = (gate * (1.0 / (1.0 + jnp.exp(-gate))) * up).astype(BF16)
            ffn = _dot(hid, w2_ref[cols, :])
            acc = ffn if acc is None else acc + ffn
        x2 = x1 + acc * r
        if final_norm:
            x2 = _unit_rms(x2) * gfin_ref[...]
        out_ref[local] = x2.reshape(D_ROWS, SUBLANES, D_MODEL)


def _mix_out(layer, x, yac, t, tabs, wout, w13, w2, gfin, final_norm):
    x3 = x.reshape(RADIX, RADIX, D_MODEL)
    yac3 = yac.reshape(RADIX, RADIX, YAC_DIM)
    const = lambda s: (0, 0)
    lay = lambda s: (layer, 0, 0)
    tile = lambda s: (0, s, 0)
    once = dict(pipeline_mode=pl.Buffered(1))
    out = pl.pallas_call(
        functools.partial(_mix_out_kernel, final_norm=final_norm),
        grid=(RADIX // SUBLANES,),
        in_specs=[
            pl.BlockSpec((RADIX, SUBLANES, D_MODEL), tile),
            pl.BlockSpec((RADIX, SUBLANES, YAC_DIM), tile),
            pl.BlockSpec((SUBLANES, RADIX, Z_DIM), lambda s: (s, 0, 0)),
            pl.BlockSpec((RADIX, RADIX), const),
            pl.BlockSpec((RADIX, RADIX), const),
            pl.BlockSpec((RADIX, RADIX), const),
            pl.BlockSpec((RADIX, RADIX), const),
            pl.BlockSpec((None, D_MODEL, D_MODEL), lay, **once),
            pl.BlockSpec((D_MODEL, 2 * D_FF), const, **once),
            pl.BlockSpec((D_FF, D_MODEL), const, **once),
            pl.BlockSpec((1, D_MODEL), const),
        ],
        out_specs=pl.BlockSpec((RADIX, SUBLANES, D_MODEL), tile),
        out_shape=jax.ShapeDtypeStruct((RADIX, RADIX, D_MODEL), F32),
        scratch_shapes=[pltpu.VMEM((OUT_SUBTILES, D_ROWS, SUBLANES, FOURIER_DIM), F32),
                        pltpu.VMEM((OUT_SUBTILES, TM_OUT, D_MODEL), BF16)],
        compiler_params=pltpu.CompilerParams(dimension_semantics=("arbitrary",),
                                             vmem_limit_bytes=VMEM_LIMIT),
        name="mix_out",
    )(x3, yac3, t, tabs["cphi"], tabs["sphi"], tabs["cth"], tabs["sth"], wout, w13, w2, gfin)
    return out.reshape(SEQ, D_MODEL)


def kernel(x, g_mix, w_in, w_conv, w_fourier, w_pool, pool_scale, w_out, g_ffn, w1, w3, w2, g_final):
    tabs = {k: jnp.asarray(v) for k, v in _tables().items()}
    wout, win, gf_col = _fold_weights(tabs["chan"], w_fourier, w_pool, pool_scale, w_out, w_in, g_mix, g_ffn)
    f_stage1 = tabs["f_stage1"].astype(BF16)

    xs = x.reshape(SEQ, D_MODEL)
    for l in range(DEPTH):
        yac, z, w13b, w2b = _mix_in(l, xs, win, w_conv, gf_col, w1, w3, w2)
        t = _dft_a(f_stage1, z)
        xs = _mix_out(l, xs, yac, t, tabs, wout, w13b, w2b, g_final[None, :],
                      final_norm=(l == DEPTH - 1))
    return xs.reshape(1, SEQ, D_MODEL)
```

```python
import functools

import numpy as np
import jax
import jax.numpy as jnp
from jax import lax
from jax.experimental import pallas as pl
from jax.experimental.pallas import tpu as pltpu

D_MODEL = 1024
SEQ = 16384
DEPTH = 2
HEAD_DIM = 64
CONV_DIM = 384
FOURIER_HEADS = 4
FOURIER_DIM = 256
POOL_WINDOWS = (2, 4, 8, 16)
POOL_GROUP_DIM = 96
POOL_DIM = 384
IN_PROJ_DIM = 3 * CONV_DIM + FOURIER_DIM + POOL_DIM
D_FF = 2816
EPS = 1e-6

RADIX = 128
YAC_DIM = CONV_DIM + POOL_DIM
Z_DIM = 2 * FOURIER_DIM
LANES = 128
SUBLANES = 8
DFT_COLS = 16

TM_IN = 1024
HALO = 8
assert all(w // 2 == 2 ** k for k, w in enumerate(POOL_WINDOWS)) and POOL_WINDOWS[-1] // 2 <= HALO
OUT_SUBTILES = 2
D_ROWS = RADIX // OUT_SUBTILES
TM_OUT = D_ROWS * SUBLANES
MXU_DIM = 256
FF_CHUNKS = (1536, 1280)
assert sum(FF_CHUNKS) == D_FF and all(w % MXU_DIM == 0 for w in FF_CHUNKS)
VMEM_LIMIT = 56 * 1024 * 1024

F32 = jnp.float32
BF16 = jnp.bfloat16


def _dot(a, b):
    return jnp.dot(a, b, preferred_element_type=F32)


def _unit_rms(v):
    ms = jnp.mean(v * v, axis=-1, keepdims=True)
    return v * lax.rsqrt(ms + EPS)


def _tables():
    r = np.arange(RADIX, dtype=np.float64)
    ang128 = 2.0 * np.pi * np.outer(r, r) / RADIX
    angs = 2.0 * np.pi * np.outer(r, r) / SEQ
    f_stage1 = np.concatenate([np.cos(ang128), -np.sin(ang128)], axis=0)
    e = np.arange(HEAD_DIM, dtype=np.float64)
    angc = 2.0 * np.pi * np.outer(e, e) / HEAD_DIM
    norm = 1.0 / np.sqrt(float(SEQ) * HEAD_DIM)
    eye = np.eye(FOURIER_HEADS)
    cbd = np.kron(eye, np.cos(angc)) * norm
    sbd = np.kron(eye, -np.sin(angc)) * norm
    return dict(
        f_stage1=f_stage1.astype(np.float32),
        cphi=np.cos(ang128).astype(np.float32), sphi=np.sin(ang128).astype(np.float32),
        cth=np.cos(angs).astype(np.float32), sth=np.sin(angs).astype(np.float32),
        chan=np.stack([cbd, sbd]).astype(np.float32),
    )


WIN_CH = 0
WIN_P = WIN_CH + 2 * CONV_DIM
WIN_F = WIN_P + POOL_DIM
WIN_B = WIN_F + Z_DIM
WIN_DIM = WIN_B + CONV_DIM
assert WIN_DIM % (2 * MXU_DIM) == 0
FOLD_PARTS = 4
FOLD_ROWS = D_MODEL // FOLD_PARTS


def _dot_3pass(a, b):
    a_hi = a.astype(BF16)
    b_hi = b.astype(BF16)
    a_lo = (a - a_hi.astype(F32)).astype(BF16)
    b_lo = (b - b_hi.astype(F32)).astype(BF16)
    return _dot(a_hi, b_hi) + (_dot(a_hi, b_lo) + _dot(a_lo, b_hi))


def _row_as_column(row):
    diagonal = (lax.broadcasted_iota(jnp.int32, (LANES, LANES), 0)
                == lax.broadcasted_iota(jnp.int32, (LANES, LANES), 1))
    blocks = [jnp.sum(jnp.where(diagonal, row[:, k:k + LANES], 0.0), axis=1, keepdims=True)
              for k in range(0, row.shape[1], LANES)]
    return jnp.concatenate(blocks, axis=0)


def _fold_kernel(chan_ref, wf_ref, wp_ref, ps_ref, wout_ref, win_ref, gmix_ref, gffn_ref,
                 woe_ref, wine_ref, gffn_col_ref, gain_ref, ab_ref, wfbd_ref, wpbd_ref):
    layer = pl.program_id(0)
    part = pl.program_id(1)

    @pl.when(part == 0)
    def _():
        gain_ref[...] = _row_as_column(gmix_ref[pl.ds(layer, 1), :])
        gffn_col_ref[...] = _row_as_column(gffn_ref[pl.ds(layer, 1), :])
        wfbd_ref[...] = jnp.zeros(wfbd_ref.shape, F32)
        for hd in range(FOURIER_HEADS):
            lo = hd * HEAD_DIM
            wfbd_ref[lo:lo + HEAD_DIM, lo:lo + HEAD_DIM] = wf_ref[hd]
        wpbd_ref[...] = jnp.zeros(wpbd_ref.shape, F32)
        for grp in range(len(POOL_WINDOWS)):
            lo = grp * POOL_GROUP_DIM
            wpbd_ref[lo:lo + POOL_GROUP_DIM, lo:lo + POOL_GROUP_DIM] = wp_ref[grp]
        wf = wfbd_ref[...]
        ab_ref[:, 0:FOURIER_DIM] = _dot_3pass(chan_ref[0], wf)
        ab_ref[:, FOURIER_DIM:Z_DIM] = _dot_3pass(chan_ref[1], wf)

    mixed = CONV_DIM + FOURIER_DIM
    woe_ref[0:CONV_DIM, :] = wout_ref[0:CONV_DIM, :].astype(BF16)
    woe_ref[CONV_DIM:YAC_DIM, :] = _dot_3pass(wpbd_ref[...] * ps_ref[pl.ds(layer, 1), :],
                                              wout_ref[mixed:D_MODEL, :]).astype(BF16)
    woe_ref[YAC_DIM:D_MODEL, :] = wout_ref[CONV_DIM:mixed, :].astype(BF16)
    c0 = CONV_DIM
    gain = gain_ref[pl.ds(pl.multiple_of(part * FOLD_ROWS, FOLD_ROWS), FOLD_ROWS), :]
    wine_ref[:, WIN_CH:WIN_CH + 2 * c0] = (win_ref[:, c0:3 * c0] * gain).astype(BF16)
    wine_ref[:, WIN_P:WIN_P + POOL_DIM] = (win_ref[:, 3 * c0 + FOURIER_DIM:IN_PROJ_DIM] * gain).astype(BF16)
    wine_ref[:, WIN_B:WIN_B + c0] = (win_ref[:, 0:c0] * gain).astype(BF16)
    wfour = win_ref[:, 3 * c0:3 * c0 + FOURIER_DIM] * gain
    wine_ref[:, WIN_F:WIN_F + Z_DIM] = _dot_3pass(wfour, ab_ref[...]).astype(BF16)


def _fold_weights(chan, w_fourier, w_pool, pool_scale, w_out, w_in, g_mix, g_ffn):
    lay = lambda l, q: (l, 0, 0)
    lay4 = lambda l, q: (l, 0, 0, 0)
    whole = lambda l, q: (0, 0)
    rows = lambda l, q: (l, q, 0)
    cols = lambda l, q: (l, 0, q)
    return pl.pallas_call(
        _fold_kernel,
        grid=(DEPTH, FOLD_PARTS),
        in_specs=[pl.BlockSpec((2, FOURIER_DIM, FOURIER_DIM), lambda l, q: (0, 0, 0)),
                  pl.BlockSpec((None, FOURIER_HEADS, HEAD_DIM, HEAD_DIM), lay4),
                  pl.BlockSpec((None, len(POOL_WINDOWS), POOL_GROUP_DIM, POOL_GROUP_DIM), lay4),
                  pl.BlockSpec((DEPTH, POOL_DIM), whole),
                  pl.BlockSpec((None, D_MODEL, FOLD_ROWS), cols),
                  pl.BlockSpec((None, FOLD_ROWS, IN_PROJ_DIM), rows),
                  pl.BlockSpec((DEPTH, D_MODEL), whole),
                  pl.BlockSpec((DEPTH, D_MODEL), whole)],
        out_specs=[pl.BlockSpec((None, D_MODEL, FOLD_ROWS), cols),
                   pl.BlockSpec((None, FOLD_ROWS, WIN_DIM), rows),
                   pl.BlockSpec((None, D_MODEL, 1), lay)],
        out_shape=[jax.ShapeDtypeStruct((DEPTH, D_MODEL, D_MODEL), BF16),
                   jax.ShapeDtypeStruct((DEPTH, D_MODEL, WIN_DIM), BF16),
                   jax.ShapeDtypeStruct((DEPTH, D_MODEL, 1), F32)],
        scratch_shapes=[pltpu.VMEM((D_MODEL, 1), F32), pltpu.VMEM((FOURIER_DIM, Z_DIM), F32),
                        pltpu.VMEM((FOURIER_DIM, FOURIER_DIM), F32), pltpu.VMEM((POOL_DIM, POOL_DIM), F32)],
        compiler_params=pltpu.CompilerParams(dimension_semantics=("arbitrary", "arbitrary")),
        name="fold_weights",
    )(chan, w_fourier, w_pool, pool_scale, w_out, w_in, g_mix, g_ffn)


def _window_sums(p, rows, max_half):
    sums = {1: p + pltpu.roll(p, 1, 0)}
    h = 1
    while h < max_half:
        sums[2 * h] = pltpu.roll(sums[h], h, 0) + pltpu.roll(sums[h], rows - h, 0)
        h *= 2
    return sums


def _mix_in_kernel(x_ref, xprev_ref, xnext_ref, win_ref, wconv_ref, gffn_ref, w1_ref, w3_ref, w2_ref,
                   yac_ref, z_ref, w13b_ref, w2b_ref, xe_ref):
    gffn = gffn_ref[...]
    start = 0
    for width in FF_CHUNKS:
        w13b_ref[:, 2 * start:2 * start + width] = (w1_ref[:, start:start + width] * gffn).astype(BF16)
        w13b_ref[:, 2 * start + width:2 * (start + width)] = (w3_ref[:, start:start + width] * gffn).astype(BF16)
        start += width
    w2b_ref[...] = w2_ref[...].astype(BF16)

    i = pl.program_id(0)
    n = pl.num_programs(0)
    rows = TM_IN + 2 * HALO
    prev = jnp.where(i > 0, _unit_rms(xprev_ref[...]), 0.0)
    nxt = jnp.where(i < n - 1, _unit_rms(xnext_ref[...]), 0.0)
    xe_ref[0:TM_IN, :] = _unit_rms(x_ref[...]).astype(BF16)
    xe_ref[TM_IN:rows, :] = jnp.concatenate([nxt, prev], axis=0).astype(BF16)

    z = _dot(xe_ref[...], win_ref[...])
    c0 = CONV_DIM
    tile = slice(0, TM_IN)
    zp = z[:, WIN_P:WIN_P + POOL_DIM]
    z_ref[...] = z[tile, WIN_F:WIN_F + Z_DIM].astype(BF16)

    u = z[:, WIN_CH:WIN_CH + c0] * z[:, WIN_CH + c0:WIN_CH + 2 * c0]
    conv = (wconv_ref[0:1, :] * pltpu.roll(u, 1, 0) + wconv_ref[1:2, :] * u
            + wconv_ref[2:3, :] * pltpu.roll(u, rows - 1, 0))
    yac_ref[:, 0:c0] = z[tile, WIN_B:WIN_B + c0] * conv[tile]

    lane_tiles = []
    for lo in range(0, POOL_DIM, LANES):
        p = zp[:, lo:lo + LANES]
        groups = range(lo // POOL_GROUP_DIM, (lo + LANES - 1) // POOL_GROUP_DIM + 1)
        sums = _window_sums(p, rows, POOL_WINDOWS[groups[-1]] // 2)
        lane = lo + lax.broadcasted_iota(jnp.int32, (1, LANES), 1)
        wsum = sums[POOL_WINDOWS[groups[-1]] // 2][tile]
        half = jnp.full((1, LANES), POOL_WINDOWS[groups[-1]] // 2, jnp.int32)
        for grp in reversed(groups[:-1]):
            inside = lane < (grp + 1) * POOL_GROUP_DIM
            wsum = jnp.where(inside, sums[POOL_WINDOWS[grp] // 2][tile], wsum)
            half = jnp.where(inside, POOL_WINDOWS[grp] // 2, half)
        cols = slice(c0 + lo, c0 + lo + LANES)
        yac_ref[:, cols] = wsum * (1.0 / (2 * half).astype(F32)) - p[tile]
        lane_tiles.append((cols, wsum, half, p))

    def clipped_window_rows(local, first_t):
        t = first_t + lax.broadcasted_iota(jnp.int32, (HALO, LANES), 0)
        for cols, wsum, half, p in lane_tiles:
            cnt = (jnp.minimum(t + half, SEQ) - jnp.maximum(t - half, 0)).astype(F32)
            yac_ref[local, cols] = wsum[local] / cnt - p[local]

    @pl.when(i == 0)
    def _():
        clipped_window_rows(slice(0, HALO), 0)

    @pl.when(i == n - 1)
    def _():
        clipped_window_rows(slice(TM_IN - HALO, TM_IN), SEQ - HALO)


def _mix_in(layer, x, win, wconv, gffn_col, w1, w3, w2):
    n = SEQ // TM_IN
    hb = TM_IN // HALO
    lay = lambda i: (layer, 0, 0)
    w_rows = lambda i: (layer, i, 0)
    cast_rows = lambda i: (i, 0)
    return pl.pallas_call(
        _mix_in_kernel,
        grid=(n,),
        in_specs=[
            pl.BlockSpec((TM_IN, D_MODEL), lambda i: (i, 0)),
            pl.BlockSpec((HALO, D_MODEL), lambda i: (jnp.maximum(i * hb - 1, 0), 0)),
            pl.BlockSpec((HALO, D_MODEL), lambda i: (jnp.minimum((i + 1) * hb, SEQ // HALO - 1), 0)),
            pl.BlockSpec((None, D_MODEL, WIN_DIM), lay, pipeline_mode=pl.Buffered(1)),
            pl.BlockSpec((None, 3, CONV_DIM), lay),
            pl.BlockSpec((None, D_MODEL // n, 1), w_rows),
            pl.BlockSpec((None, D_MODEL // n, D_FF), w_rows),
            pl.BlockSpec((None, D_MODEL // n, D_FF), w_rows),
            pl.BlockSpec((None, D_FF // n, D_MODEL), w_rows),
        ],
        out_specs=[pl.BlockSpec((TM_IN, YAC_DIM), lambda i: (i, 0)),
                   pl.BlockSpec((TM_IN, Z_DIM), lambda i: (i, 0)),
                   pl.BlockSpec((D_MODEL // n, 2 * D_FF), cast_rows),
                   pl.BlockSpec((D_FF // n, D_MODEL), cast_rows)],
        out_shape=[jax.ShapeDtypeStruct((SEQ, YAC_DIM), F32),
                   jax.ShapeDtypeStruct((SEQ, Z_DIM), BF16),
                   jax.ShapeDtypeStruct((D_MODEL, 2 * D_FF), BF16),
                   jax.ShapeDtypeStruct((D_FF, D_MODEL), BF16)],
        scratch_shapes=[pltpu.VMEM((TM_IN + 2 * HALO, D_MODEL), BF16)],
        compiler_params=pltpu.CompilerParams(dimension_semantics=("arbitrary",),
                                             vmem_limit_bytes=VMEM_LIMIT),
        name="mix_in",
    )(x, x, x, win, wconv, gffn_col, w1, w3, w2)


def _dft_a_kernel(f_ref, z_ref, t_ref, zslab_ref, tslab_ref):
    h = FOURIER_DIM
    f = f_ref[...]
    zslab_ref[...] = jnp.swapaxes(z_ref[...].astype(F32), 0, 1)
    for j in range(DFT_COLS):
        res = _dot(f, zslab_ref[j].astype(BF16))
        tslab_ref[j, :, 0:h] = res[0:RADIX, 0:h] - res[RADIX:2 * RADIX, h:Z_DIM]
        tslab_ref[j, :, h:Z_DIM] = res[0:RADIX, h:Z_DIM] + res[RADIX:2 * RADIX, 0:h]
    t_ref[...] = jnp.swapaxes(tslab_ref[...], 0, 1).astype(BF16)


def _dft_a(f_stage1, z):
    z3 = z.reshape(RADIX, RADIX, Z_DIM)
    block = (RADIX, DFT_COLS, Z_DIM)
    return pl.pallas_call(
        _dft_a_kernel,
        grid=(RADIX // DFT_COLS,),
        in_specs=[pl.BlockSpec((2 * RADIX, RADIX), lambda t: (0, 0)),
                  pl.BlockSpec(block, lambda t: (0, t, 0))],
        out_specs=pl.BlockSpec(block, lambda t: (0, t, 0)),
        out_shape=jax.ShapeDtypeStruct((RADIX, RADIX, Z_DIM), BF16),
        scratch_shapes=[pltpu.VMEM((DFT_COLS, RADIX, Z_DIM), F32), pltpu.VMEM((DFT_COLS, RADIX, Z_DIM), F32)],
        compiler_params=pltpu.CompilerParams(dimension_semantics=("arbitrary",)),
        name="dft_a",
    )(f_stage1, z3)


def _mix_out_kernel(x_ref, yac_ref, t_ref, cphi_ref, sphi_ref, cth_ref, sth_ref, wout_ref,
                    w13_ref, w2_ref, gfin_ref, out_ref, yb_ref, ybuf, *, final_norm):
    chunk = pl.program_id(0)
    h = FOURIER_DIM

    for sub in range(OUT_SUBTILES):
        local = slice(sub * D_ROWS, (sub + 1) * D_ROWS)
        cphi = cphi_ref[local, :]
        sphi = sphi_ref[local, :]
        for j in range(SUBLANES):
            c = chunk * SUBLANES + j
            cth = cth_ref[pl.ds(c, 1), :]
            sth = sth_ref[pl.ds(c, 1), :]
            mcos = (cphi * cth - sphi * sth).astype(BF16)
            msin = (sphi * cth + cphi * sth).astype(BF16)
            tj = t_ref[j]
            yb_ref[sub, :, j, :] = _dot(jnp.concatenate([mcos, msin], axis=1),
                                        jnp.concatenate([tj[:, 0:h], tj[:, h:Z_DIM]], axis=0))

        yac = yac_ref[local].reshape(TM_OUT, YAC_DIM)
        yb = yb_ref[sub].reshape(TM_OUT, h)
        ybuf[sub, :, 0:YAC_DIM] = yac.astype(BF16)
        ybuf[sub, :, YAC_DIM:D_MODEL] = yb.astype(BF16)

        x1 = x_ref[local].reshape(TM_OUT, D_MODEL) + _dot(ybuf[sub], wout_ref[...])
        xg = x1.astype(BF16)
        r = lax.rsqrt(jnp.mean(x1 * x1, axis=-1, keepdims=True) + EPS)
        acc = None
        start = 0
        for width in FF_CHUNKS:
            cols = slice(start, start + width)
            start += width
            both = _dot(xg, w13_ref[:, 2 * cols.start:2 * cols.stop])
            gate = both[:, 0:width] * r
            up = both[:, width:2 * width]
            hid = (gate * (1.0 / (1.0 + jnp.exp(-gate))) * up).astype(BF16)
            ffn = _dot(hid, w2_ref[cols, :])
            acc = ffn if acc is None else acc + ffn
        x2 = x1 + acc * r
        if final_norm:
            x2 = _unit_rms(x2) * gfin_ref[...]
        out_ref[local] = x2.reshape(D_ROWS, SUBLANES, D_MODEL)


def _mix_out(layer, x, yac, t, tabs, wout, w13, w2, gfin, final_norm):
    x3 = x.reshape(RADIX, RADIX, D_MODEL)
    yac3 = yac.reshape(RADIX, RADIX, YAC_DIM)
    const = lambda s: (0, 0)
    lay = lambda s: (layer, 0, 0)
    tile = lambda s: (0, s, 0)
    once = dict(pipeline_mode=pl.Buffered(1))
    out = pl.pallas_call(
        functools.partial(_mix_out_kernel, final_norm=final_norm),
        grid=(RADIX // SUBLANES,),
        in_specs=[
            pl.BlockSpec((RADIX, SUBLANES, D_MODEL), tile),
            pl.BlockSpec((RADIX, SUBLANES, YAC_DIM), tile),
            pl.BlockSpec((SUBLANES, RADIX, Z_DIM), lambda s: (s, 0, 0)),
            pl.BlockSpec((RADIX, RADIX), const),
            pl.BlockSpec((RADIX, RADIX), const),
            pl.BlockSpec((RADIX, RADIX), const),
            pl.BlockSpec((RADIX, RADIX), const),
            pl.BlockSpec((None, D_MODEL, D_MODEL), lay, **once),
            pl.BlockSpec((D_MODEL, 2 * D_FF), const, **once),
            pl.BlockSpec((D_FF, D_MODEL), const, **once),
            pl.BlockSpec((1, D_MODEL), const),
        ],
        out_specs=pl.BlockSpec((RADIX, SUBLANES, D_MODEL), tile),
        out_shape=jax.ShapeDtypeStruct((RADIX, RADIX, D_MODEL), F32),
        scratch_shapes=[pltpu.VMEM((OUT_SUBTILES, D_ROWS, SUBLANES, FOURIER_DIM), F32),
                        pltpu.VMEM((OUT_SUBTILES, TM_OUT, D_MODEL), BF16)],
        compiler_params=pltpu.CompilerParams(dimension_semantics=("arbitrary",),
                                             vmem_limit_bytes=VMEM_LIMIT),
        name="mix_out",
    )(x3, yac3, t, tabs["cphi"], tabs["sphi"], tabs["cth"], tabs["sth"], wout, w13, w2, gfin)
    return out.reshape(SEQ, D_MODEL)


def kernel(x, g_mix, w_in, w_conv, w_fourier, w_pool, pool_scale, w_out, g_ffn, w1, w3, w2, g_final):
    tabs = {k: jnp.asarray(v) for k, v in _tables().items()}
    wout, win, gf_col = _fold_weights(tabs["chan"], w_fourier, w_pool, pool_scale, w_out, w_in, g_mix, g_ffn)
    f_stage1 = tabs["f_stage1"].astype(BF16)

    xs = x.reshape(SEQ, D_MODEL)
    for l in range(DEPTH):
        yac, z, w13b, w2b = _mix_in(l, xs, win, w_conv, gf_col, w1, w3, w2)
        t = _dft_a(f_stage1, z)
        xs = _mix_out(l, xs, yac, t, tabs, wout, w13b, w2b, g_final[None, :],
                      final_norm=(l == DEPTH - 1))
    return xs.reshape(1, SEQ, D_MODEL)
```
